```python
import jax, jax.numpy as jnp
from jax import lax
import numpy as np


D_MODEL = 1024
BATCH = 8
SEQ = 4096
DEPTH = 2

CTX_LEN = 256
GRID_W = 64
HEAD_DIM = 64
NA_HEADS = 6
NA_WIN_H = 8
NA_WIN_W = 16
NA_QBLK_W = 16
NA_KBLK_W = NA_QBLK_W + NA_WIN_W
SW_HEADS = 6
SW_KV_HEADS = 2
SW_WINDOW = 128
SW_BLOCK = 128
CONV_DIM = 256
CONV_WIDTH = 31
D_FF = 2816
ROPE_BASE = 10000.0
EPS = 1e-6
NEG_INF = -1e30
N_BRANCH = 3
N_MOD = 9

A_DIM = NA_HEADS * HEAD_DIM
B_Q_DIM = SW_HEADS * HEAD_DIM
B_KV_DIM = SW_KV_HEADS * HEAD_DIM
IN_DIM = 3 * A_DIM + B_Q_DIM + 2 * B_KV_DIM + 2 * CONV_DIM + N_BRANCH * D_MODEL
SPLIT_POINTS = (A_DIM, 2 * A_DIM, 3 * A_DIM, 3 * A_DIM + B_Q_DIM,
                3 * A_DIM + B_Q_DIM + B_KV_DIM, 3 * A_DIM + B_Q_DIM + 2 * B_KV_DIM,
                3 * A_DIM + B_Q_DIM + 2 * B_KV_DIM + 2 * CONV_DIM)

kernel_name = 'hybrid_natten_swa_conformer_prefix_dit_block'


def rms_norm(x, g):
    xf = x.astype(jnp.float32)
    y = xf * lax.rsqrt(jnp.mean(xf * xf, axis=-1, keepdims=True) + EPS)
    return (y * g.astype(jnp.float32)).astype(x.dtype)


def layer_norm(x, g, b):
    xf = x.astype(jnp.float32)
    mu = jnp.mean(xf, axis=-1, keepdims=True)
    var = jnp.mean(jnp.square(xf - mu), axis=-1, keepdims=True)
    y = (xf - mu) * lax.rsqrt(var + EPS) * g.astype(jnp.float32) + b.astype(jnp.float32)
    return y.astype(x.dtype)


def modulate(h, shift, scale):
    return h * (1 + scale) + shift


def swiglu(h, w_gu, w_down):
    a, b = jnp.split(h @ w_gu, 2, axis=-1)
    return (jax.nn.silu(a) * b) @ w_down


def axial_rope_tables(n_tokens):
    t = jnp.arange(n_tokens, dtype=jnp.int32)
    row = (t // GRID_W).astype(jnp.float32)
    col = (t % GRID_W).astype(jnp.float32)
    n_freq = HEAD_DIM // 4
    inv_freq = ROPE_BASE ** (-jnp.arange(n_freq, dtype=jnp.float32) / n_freq)
    ang = jnp.concatenate([row[:, None] * inv_freq, col[:, None] * inv_freq], axis=-1)
    return jnp.cos(ang), jnp.sin(ang)


def apply_axial_rope(x, cos, sin):
    half = HEAD_DIM // 2
    xf = x.astype(jnp.float32)
    x1, x2 = xf[..., :half], xf[..., half:]
    cs, sn = cos[None, :, None, :], sin[None, :, None, :]
    return jnp.concatenate([x1 * cs - x2 * sn, x1 * sn + x2 * cs], axis=-1).astype(x.dtype)


def split_combined(z):
    bsz, n, _ = z.shape
    qa, ka, va, qb, kb, vb, u, gates = jnp.split(z, SPLIT_POINTS, axis=-1)
    heads = lambda t, h: t.reshape(bsz, n, h, HEAD_DIM)
    return (heads(qa, NA_HEADS), heads(ka, NA_HEADS), heads(va, NA_HEADS),
            heads(qb, SW_HEADS), heads(kb, SW_KV_HEADS), heads(vb, SW_KV_HEADS), u, gates)


def neighborhood_attention(q, k, v, k_ctx, v_ctx, rpb, rows):
    bsz, n, h, d = q.shape
    n_ctx = k_ctx.shape[1]
    kh = min(NA_WIN_H, rows)
    ncb = GRID_W // NA_QBLK_W
    scale = d ** -0.5
    r = jnp.arange(rows)
    key_rows = jnp.clip(r - kh // 2, 0, rows - kh)[:, None] + jnp.arange(kh)[None]
    j = jnp.arange(ncb)
    blk_c0 = jnp.clip(j * NA_QBLK_W - NA_WIN_W // 2, 0, GRID_W - NA_KBLK_W)
    key_cols = blk_c0[:, None] + jnp.arange(NA_KBLK_W)[None]
    qg = q.reshape(bsz, rows, ncb, NA_QBLK_W, h, d)
    kg = k.reshape(bsz, rows, GRID_W, h, d)
    vg = v.reshape(bsz, rows, GRID_W, h, d)
    idx_r = key_rows[:, None, :, None]
    idx_c = key_cols[None, :, None, :]
    kb = kg[:, idx_r, idx_c]
    vb = vg[:, idx_r, idx_c]
    s_loc = jnp.einsum('brjqhd,brjyxhd->bhrjqyx', qg, kb, preferred_element_type=jnp.float32) * scale
    qcol = j[:, None] * NA_QBLK_W + jnp.arange(NA_QBLK_W)[None]
    win_c0 = jnp.clip(qcol - NA_WIN_W // 2, 0, GRID_W - NA_WIN_W)[..., None]
    kc = key_cols[:, None, :]
    in_win = (kc >= win_c0) & (kc < win_c0 + NA_WIN_W)
    dyi = key_rows - r[:, None] + (NA_WIN_H - 1)
    dxi = jnp.clip(kc - qcol[..., None] + (NA_WIN_W - 1), 0, 2 * NA_WIN_W - 2)
    bias = rpb[:, dyi[:, None, None, :, None], dxi[None, :, :, None, :]]
    s_loc = jnp.where(in_win[None, None, None, :, :, None, :], s_loc + bias.astype(jnp.float32)[None], NEG_INF)
    s_ctx = jnp.einsum('bshd,bchd->bhsc', q, k_ctx, preferred_element_type=jnp.float32) * scale
    s_ctx = s_ctx.reshape(bsz, h, rows, ncb, NA_QBLK_W, n_ctx)
    n_loc = kh * NA_KBLK_W
    p = jax.nn.softmax(jnp.concatenate([s_loc.reshape(bsz, h, rows, ncb, NA_QBLK_W, n_loc), s_ctx], axis=-1), axis=-1)
    p = p.astype(v.dtype)
    p_loc = p[..., :n_loc].reshape(bsz, h, rows, ncb, NA_QBLK_W, kh, NA_KBLK_W)
    o = (jnp.einsum('bhrjqyx,brjyxhd->brjqhd', p_loc, vb)
         + jnp.einsum('bhrjqc,bchd->brjqhd', p[..., n_loc:], v_ctx))
    return o.reshape(bsz, n, h * d)


def window_attention(q, k, v, k_ctx, v_ctx, sink):
    bsz, n, h, d = q.shape
    hkv = k.shape[2]
    g = h // hkv
    nb = n // SW_BLOCK
    scale = d ** -0.5
    pad = ((0, 0), (SW_BLOCK, SW_BLOCK), (0, 0), (0, 0))
    kp, vp = jnp.pad(k, pad), jnp.pad(v, pad)
    blk_idx = jnp.arange(nb)[:, None] * SW_BLOCK + jnp.arange(3 * SW_BLOCK)[None]
    kb, vb = kp[:, blk_idx], vp[:, blk_idx]
    qb = q.reshape(bsz, nb, SW_BLOCK, hkv, g, d)
    s_loc = jnp.einsum('bnqkgd,bnckd->bkgnqc', qb, kb, preferred_element_type=jnp.float32) * scale
    qpos = jnp.arange(n).reshape(nb, SW_BLOCK)
    kpos = blk_idx - SW_BLOCK
    valid = (jnp.abs(qpos[:, :, None] - kpos[:, None, :]) <= SW_WINDOW) & (kpos >= 0)[:, None, :] & (kpos < n)[:, None, :]
    s_loc = jnp.where(valid[None, None, None], s_loc, NEG_INF)
    s_ctx = jnp.einsum('bnqkgd,bmkd->bkgnqm', qb, k_ctx, preferred_element_type=jnp.float32) * scale
    s_sink = jnp.broadcast_to(sink.astype(jnp.float32).reshape(1, hkv, g, 1, 1, 1), s_loc.shape[:-1] + (1,))
    p = jax.nn.softmax(jnp.concatenate([s_loc, s_ctx, s_sink], axis=-1), axis=-1).astype(v.dtype)
    n_loc = 3 * SW_BLOCK
    n_ctx = k_ctx.shape[1]
    o = (jnp.einsum('bkgnqc,bnckd->bnqkgd', p[..., :n_loc], vb)
         + jnp.einsum('bkgnqm,bmkd->bnqkgd', p[..., n_loc:n_loc + n_ctx], v_ctx))
    return o.reshape(bsz, n, h * d)


def context_attention(q, k, v, sink):
    bsz, n_ctx, h, d = q.shape
    hkv = k.shape[2]
    g = h // hkv
    qg = q.reshape(bsz, n_ctx, hkv, g, d)
    s = jnp.einsum('bqkgd,bckd->bkgqc', qg, k, preferred_element_type=jnp.float32) * d ** -0.5
    if sink is None:
        p = jax.nn.softmax(s, axis=-1)
    else:
        s_sink = jnp.broadcast_to(sink.astype(jnp.float32).reshape(1, hkv, g, 1, 1), s.shape[:-1] + (1,))
        p = jax.nn.softmax(jnp.concatenate([s, s_sink], axis=-1), axis=-1)[..., :n_ctx]
    o = jnp.einsum('bkgqc,bckd->bqkgd', p.astype(v.dtype), v)
    return o.reshape(bsz, n_ctx, h * d)


def conformer_conv(u, dw_w, dw_b, ln_g, ln_b):
    a, gt = jnp.split(u, 2, axis=-1)
    h = a * jax.nn.sigmoid(gt)
    h = lax.conv_general_dilated(h, dw_w.astype(h.dtype)[:, None, :], window_strides=(1,),
                                 padding=((CONV_WIDTH // 2, CONV_WIDTH // 2),),
                                 dimension_numbers=('NWC', 'WIO', 'NWC'),
                                 feature_group_count=CONV_DIM) + dw_b
    return jax.nn.silu(layer_norm(h, ln_g, ln_b))


def merge_branches(ya, yb, yc, gate_logits, b_gate, w_oa, w_ob, w_oc, w_o):
    ga, gb, gc = jnp.split(jax.nn.sigmoid(gate_logits + b_gate), N_BRANCH, axis=-1)
    return (ga * (ya @ w_oa) + gb * (yb @ w_ob) + gc * (yc @ w_oc)) @ w_o


def setup_inputs(seed: int = 0) -> dict:
    key = jax.random.key(seed)
    ks = jax.random.split(key, 32)
    f32 = jnp.float32
    D = D_MODEL
    nrm = lambda k, shape, fan_in: jax.random.normal(k, shape, f32) * fan_in ** -0.5
    small = lambda k, shape, s: jax.random.normal(k, shape, f32) * s
    return {
        'x': jax.random.normal(ks[0], (BATCH, SEQ, D), f32),
        'c': jax.random.normal(ks[1], (BATCH, D), f32),
        'ctx': jax.random.normal(ks[2], (BATCH, CTX_LEN, D), f32),
        'c_ctx': jax.random.normal(ks[3], (D,), f32),
        'w_ada': nrm(ks[4], (DEPTH, D, N_MOD * D), D),
        'b_ada': small(ks[5], (DEPTH, N_MOD * D), 0.02),
        'norm_g': 1.0 + small(ks[6], (DEPTH, 3, D), 0.05),
        'w_ffn1_gu': nrm(ks[7], (DEPTH, D, 2 * D_FF), D),
        'w_ffn1_down': nrm(ks[8], (DEPTH, D_FF, D), D_FF),
        'w_ffn2_gu': nrm(ks[9], (DEPTH, D, 2 * D_FF), D),
        'w_ffn2_down': nrm(ks[10], (DEPTH, D_FF, D), D_FF),
        'w_in': nrm(ks[11], (DEPTH, D, IN_DIM), D),
        'b_gate': small(ks[12], (DEPTH, N_BRANCH * D), 0.02),
        'na_rpb': small(ks[13], (DEPTH, NA_HEADS, 2 * NA_WIN_H - 1, 2 * NA_WIN_W - 1), 0.1),
        'sw_sink': small(ks[14], (DEPTH, SW_HEADS), 0.5),
        'conv_dw_w': nrm(ks[15], (DEPTH, CONV_WIDTH, CONV_DIM), CONV_WIDTH),
        'conv_dw_b': small(ks[16], (DEPTH, CONV_DIM), 0.02),
        'conv_ln_g': 1.0 + small(ks[17], (DEPTH, CONV_DIM), 0.05),
        'conv_ln_b': small(ks[18], (DEPTH, CONV_DIM), 0.02),
        'w_out_a': nrm(ks[19], (DEPTH, A_DIM, D), A_DIM),
        'w_out_b': nrm(ks[20], (DEPTH, B_Q_DIM, D), B_Q_DIM),
        'w_out_c': nrm(ks[21], (DEPTH, CONV_DIM, D), CONV_DIM),
        'w_out': nrm(ks[22], (DEPTH, D, D), D),
        'final_g': 1.0 + small(ks[23], (D,), 0.05),
    }


def reference(x, c, ctx, c_ctx, w_ada, b_ada, norm_g, w_ffn1_gu, w_ffn1_down, w_ffn2_gu, w_ffn2_down,
              w_in, b_gate, na_rpb, sw_sink, conv_dw_w, conv_dw_b, conv_ln_g, conv_ln_b,
              w_out_a, w_out_b, w_out_c, w_out, final_g):
    n_tok = x.shape[1]
    rows = n_tok // GRID_W
    cos, sin = axial_rope_tables(n_tok)
    h_ctx = ctx
    silu_c = jax.nn.silu(c)
    silu_cc = jax.nn.silu(c_ctx)
    for l in range(DEPTH):
        last = l == DEPTH - 1
        mx = jnp.split((silu_c @ w_ada[l] + b_ada[l])[:, None, :], N_MOD, axis=-1)
        mc = jnp.split((silu_cc @ w_ada[l] + b_ada[l])[None, None, :], N_MOD, axis=-1)
        x = x + 0.5 * mx[2] * swiglu(modulate(rms_norm(x, norm_g[l, 0]), mx[0], mx[1]), w_ffn1_gu[l], w_ffn1_down[l])
        h_ctx = h_ctx + 0.5 * mc[2] * swiglu(modulate(rms_norm(h_ctx, norm_g[l, 0]), mc[0], mc[1]), w_ffn1_gu[l], w_ffn1_down[l])
        zx = modulate(rms_norm(x, norm_g[l, 1]), mx[3], mx[4]) @ w_in[l]
        zc = modulate(rms_norm(h_ctx, norm_g[l, 1]), mc[3], mc[4]) @ w_in[l]
        qa, ka, va, qb, kb, vb, ux, gx = split_combined(zx)
        qa_c, ka_c, va_c, qb_c, kb_c, vb_c, uc, gc = split_combined(zc)
        qb = apply_axial_rope(qb, cos, sin)
        kb = apply_axial_rope(kb, cos, sin)
        ya = neighborhood_attention(qa, ka, va, ka_c, va_c, na_rpb[l], rows)
        yb = window_attention(qb, kb, vb, kb_c, vb_c, sw_sink[l])
        yc = conformer_conv(ux, conv_dw_w[l], conv_dw_b[l], conv_ln_g[l], conv_ln_b[l])
        x = x + mx[5] * merge_branches(ya, yb, yc, gx, b_gate[l], w_out_a[l], w_out_b[l], w_out_c[l], w_out[l])
        if not last:
            ya_c = context_attention(qa_c, ka_c, va_c, None)
            yb_c = context_attention(qb_c, kb_c, vb_c, sw_sink[l])
            yc_c = conformer_conv(uc, conv_dw_w[l], conv_dw_b[l], conv_ln_g[l], conv_ln_b[l])
            h_ctx = h_ctx + mc[5] * merge_branches(ya_c, yb_c, yc_c, gc, b_gate[l], w_out_a[l], w_out_b[l], w_out_c[l], w_out[l])
        x = x + 0.5 * mx[8] * swiglu(modulate(rms_norm(x, norm_g[l, 2]), mx[6], mx[7]), w_ffn2_gu[l], w_ffn2_down[l])
        if not last:
            h_ctx = h_ctx + 0.5 * mc[8] * swiglu(modulate(rms_norm(h_ctx, norm_g[l, 2]), mc[6], mc[7]), w_ffn2_gu[l], w_ffn2_down[l])
    return rms_norm(x, final_g)
```

```python
import functools

import jax
import jax.numpy as jnp
import numpy as np
from jax import lax
from jax.experimental import pallas as pl
from jax.experimental.pallas import tpu as pltpu

D_MODEL = 1024
GRID_W = 64
HEAD_DIM = 64
NA_HEADS = 6
NA_WIN_H = 8
NA_WIN_W = 16
SW_HEADS = 6
SW_KV_HEADS = 2
SW_WINDOW = 128
SW_BLOCK = 128
CONV_DIM = 256
CONV_WIDTH = 31
D_FF = 2816
ROPE_BASE = 10000.0
EPS = 1e-6
NEG_INF = -1e30
N_MOD = 9

A_DIM = NA_HEADS * HEAD_DIM
B_Q_DIM = SW_HEADS * HEAD_DIM
B_KV_DIM = SW_KV_HEADS * HEAD_DIM
GATE_DIM = 3 * D_MODEL

Z_QA, Z_KA, Z_VA, Z_QB = 0, A_DIM, 2 * A_DIM, 3 * A_DIM
Z_U = 3 * A_DIM + B_Q_DIM
Z_KB = Z_U + 2 * CONV_DIM
Z_VB = Z_KB + B_KV_DIM
Z_DIM = Z_VB + B_KV_DIM

LANES = 128
SUBLANES = 8
MXU_DIM = 256
VMEM_BYTES_V7X = 64 * 1024 * 1024
VMEM_LIMIT = VMEM_BYTES_V7X - 8 * 1024 * 1024

TOKEN_TILE = 512
FF_CHUNKS = ((0, 768), (768, 1536), (1536, 2304), (2304, 2816))
NA_ROWS_PER_STEP = 8
CONV_CHUNK = 128
CONV_HALO = 16

BF16 = jnp.bfloat16
F32 = jnp.float32


def _params(n_axes):
    return pltpu.CompilerParams(dimension_semantics=("arbitrary",) * n_axes,
                                vmem_limit_bytes=VMEM_LIMIT)


def _const_spec(shape):
    nd = len(shape)
    return pl.BlockSpec(shape, lambda *_: (0,) * nd, pipeline_mode=pl.Buffered(1))


def _sigmoid(v):
    return 1.0 / (1.0 + jnp.exp(-v))


def _silu(v):
    return v * _sigmoid(v)


def _norm_modulate(x, g, shift, scale):
    ms = jnp.mean(x * x, axis=-1, keepdims=True)
    y = x * lax.rsqrt(ms + EPS) * g
    return y * (1.0 + scale) + shift


def _dot(a, b):
    return jnp.dot(a, b, preferred_element_type=F32)


def _dot_nt(a, b):
    return lax.dot_general(a, b, (((1,), (1,)), ((), ())), preferred_element_type=F32)


def _half_masks(dtype):
    lane = lax.broadcasted_iota(jnp.int32, (1, LANES), 1)
    lo = (lane < HEAD_DIM).astype(F32).astype(dtype)
    hi = (lane >= HEAD_DIM).astype(F32).astype(dtype)
    return lo, hi


def _ada_kernel(c_ref, w_ref, b_ref, o_ref):
    s = _silu(c_ref[...]).astype(BF16)
    o_ref[0] = _dot(s, w_ref[0].astype(BF16)) + b_ref[0]


def _ada(cvec, w_ada, b_ada):
    depth, d, n = w_ada.shape
    rows = cvec.shape[0]
    bn = 1024
    return pl.pallas_call(
        _ada_kernel,
        grid=(depth, n // bn),
        in_specs=[pl.BlockSpec((rows, d), lambda l, j: (0, 0)),
                  pl.BlockSpec((1, d, bn), lambda l, j: (l, 0, j)),
                  pl.BlockSpec((1, 1, bn), lambda l, j: (l, 0, j))],
        out_specs=pl.BlockSpec((1, rows, bn), lambda l, j: (l, 0, j)),
        out_shape=jax.ShapeDtypeStruct((depth, rows, n), F32),
        compiler_params=_params(2),
        name="ada",
    )(cvec, w_ada, b_ada.reshape(depth, 1, n))


def _ffn_kernel(x_ref, mod_ref, g_ref, wgu_ref, wd_ref, fg_ref, o_ref, *, mod_off, final):
    x = x_ref[...]
    shift = mod_ref[0, mod_off:mod_off + 1, :]
    scale = mod_ref[0, mod_off + 1:mod_off + 2, :]
    gate = mod_ref[0, mod_off + 2:mod_off + 3, :]
    h = _norm_modulate(x, g_ref[...], shift, scale).astype(BF16)
    acc = jnp.zeros(x.shape, F32)
    for c0, c1 in FF_CHUNKS:
        a = _dot(h, wgu_ref[:, c0:c1])
        b = _dot(h, wgu_ref[:, D_FF + c0:D_FF + c1])
        act = (_silu(a) * b).astype(BF16)
        acc = acc + _dot(act, wd_ref[c0:c1, :])
    y = x + (0.5 * gate) * acc
    if final:
        ms = jnp.mean(y * y, axis=-1, keepdims=True)
        y = y * lax.rsqrt(ms + EPS) * fg_ref[...]
    o_ref[...] = y


def _ffn(x, mod, g, w_gu, w_down, final_g, *, mod_off, final, tiles_per_mod):
    t, d = x.shape
    tm = TOKEN_TILE
    return pl.pallas_call(
        functools.partial(_ffn_kernel, mod_off=mod_off, final=final),
        grid=(t // tm,),
        in_specs=[pl.BlockSpec((tm, d), lambda i: (i, 0)),
                  pl.BlockSpec((1, N_MOD, d), lambda i: (i // tiles_per_mod, 0, 0)),
                  _const_spec((1, d)),
                  _const_spec(w_gu.shape),
                  _const_spec(w_down.shape),
                  _const_spec((1, d))],
        out_specs=pl.BlockSpec((tm, d), lambda i: (i, 0)),
        out_shape=jax.ShapeDtypeStruct((t, d), F32),
        compiler_params=_params(1),
        name="ffn",
    )(x, mod, g, w_gu, w_down, final_g)


def _rope_tile(t, cos, sin_signed, first_half):
    fwd = pltpu.roll(t, LANES - HEAD_DIM // 2, axis=1)
    bwd = pltpu.roll(t, HEAD_DIM // 2, axis=1)
    return t * cos + jnp.where(first_half, fwd, bwd) * sin_signed


def _inproj_kernel(x_ref, mod_ref, g_ref, w_ref, cos_ref, sin_ref, z_ref):
    x = x_ref[...]
    h = _norm_modulate(x, g_ref[...], mod_ref[0, 3:4, :], mod_ref[0, 4:5, :]).astype(BF16)
    z = _dot(h, w_ref[...])
    cos = cos_ref[...]
    sin = sin_ref[...]
    lane = lax.broadcasted_iota(jnp.int32, (1, LANES), 1)
    first_half = (lane % HEAD_DIM) < (HEAD_DIM // 2)
    z_ref[:, 0:Z_QB] = z[:, 0:Z_QB].astype(BF16)
    for c0 in range(Z_QB, Z_U, LANES):
        z_ref[:, c0:c0 + LANES] = _rope_tile(z[:, c0:c0 + LANES], cos, sin, first_half).astype(BF16)
    z_ref[:, Z_U:Z_KB] = z[:, Z_U:Z_KB].astype(BF16)
    z_ref[:, Z_KB:Z_VB] = _rope_tile(z[:, Z_KB:Z_VB], cos, sin, first_half).astype(BF16)
    z_ref[:, Z_VB:Z_DIM] = z[:, Z_VB:Z_DIM].astype(BF16)


def _inproj(x, mod, g, w_main, cos_tbl, sin_tbl, *, tiles_per_mod, table_tiles):
    t, d = x.shape
    tm = TOKEN_TILE
    return pl.pallas_call(
        _inproj_kernel,
        grid=(t // tm,),
        in_specs=[pl.BlockSpec((tm, d), lambda i: (i, 0)),
                  pl.BlockSpec((1, N_MOD, d), lambda i: (i // tiles_per_mod, 0, 0)),
                  _const_spec((1, d)),
                  _const_spec(w_main.shape),
                  pl.BlockSpec((tm, LANES), lambda i: (i % table_tiles, 0)),
                  pl.BlockSpec((tm, LANES), lambda i: (i % table_tiles, 0))],
        out_specs=pl.BlockSpec((tm, Z_DIM), lambda i: (i, 0)),
        out_shape=jax.ShapeDtypeStruct((t, Z_DIM), BF16),
        compiler_params=_params(1),
        name="inproj",
    )(x, mod, g, w_main, cos_tbl, sin_tbl)


def _softmax_pv(s_list, v_list, extra_logit=None):
    m = s_list[0].max(axis=-1, keepdims=True)
    for s in s_list[1:]:
        m = jnp.maximum(m, s.max(axis=-1, keepdims=True))
    if extra_logit is not None:
        m = jnp.maximum(m, extra_logit)
    l = None
    o = None
    for s, v in zip(s_list, v_list):
        p = jnp.exp(s - m)
        ps = p.sum(axis=-1, keepdims=True)
        pv = _dot(p.astype(BF16), v)
        l = ps if l is None else l + ps
        o = pv if o is None else o + pv
    if extra_logit is not None:
        l = l + jnp.exp(extra_logit - m)
    return o / l


def _merge_halves(o, n_tiles, rows):
    lane = lax.broadcasted_iota(jnp.int32, (1, LANES), 1)
    low = lane < HEAD_DIM
    return [jnp.where(low, o[j * rows:(j + 1) * rows], o[(n_tiles + j) * rows:(n_tiles + j + 1) * rows])
            for j in range(n_tiles)]


def _na_kernel(q_ref, k_ref, v_ref, kc_ref, vc_ref, bias_ref, o_ref, *, rows_total):
    lo, hi = _half_masks(BF16)
    step = pl.program_id(1)
    n_key = NA_WIN_H * GRID_W

    def row_body(rr, carry):
        r = step * NA_ROWS_PER_STEP + rr
        r0 = jnp.clip(r - NA_WIN_H // 2, 0, rows_total - NA_WIN_H)
        d = r - r0
        ks = pl.multiple_of(r0 * GRID_W, GRID_W)
        qs = pl.multiple_of(rr * GRID_W, GRID_W)
        for t in range(A_DIM // LANES):
            cols = slice(t * LANES, (t + 1) * LANES)
            qt = q_ref[pl.ds(qs, GRID_W), cols]
            q2 = jnp.concatenate([qt * lo, qt * hi], axis=0)
            kt = k_ref[pl.ds(ks, n_key), cols]
            vt = v_ref[pl.ds(ks, n_key), cols]
            bias = jnp.concatenate([bias_ref[2 * t, d], bias_ref[2 * t + 1, d]], axis=0)
            s_loc = _dot_nt(q2, kt) + bias
            s_ctx = _dot_nt(q2, kc_ref[:, cols])
            o = _softmax_pv([s_loc, s_ctx], [vt, vc_ref[:, cols]])
            o_ref[pl.ds(qs, GRID_W), cols] = _merge_halves(o, 1, GRID_W)[0].astype(BF16)
        return carry

    lax.fori_loop(0, NA_ROWS_PER_STEP, row_body, 0)


def _na(z_x, z_c, bias_tbl, *, batch, seq, n_ctx):
    rows_total = seq // GRID_W
    steps = rows_total // NA_ROWS_PER_STEP
    tq = NA_ROWS_PER_STEP * GRID_W
    return pl.pallas_call(
        functools.partial(_na_kernel, rows_total=rows_total),
        grid=(batch, steps),
        in_specs=[pl.BlockSpec((tq, A_DIM), lambda b, i: (b * steps + i, Z_QA // A_DIM)),
                  pl.BlockSpec((seq, A_DIM), lambda b, i: (b, Z_KA // A_DIM)),
                  pl.BlockSpec((seq, A_DIM), lambda b, i: (b, Z_VA // A_DIM)),
                  pl.BlockSpec((n_ctx, A_DIM), lambda b, i: (b, Z_KA // A_DIM)),
                  pl.BlockSpec((n_ctx, A_DIM), lambda b, i: (b, Z_VA // A_DIM)),
                  _const_spec(bias_tbl.shape)],
        out_specs=pl.BlockSpec((tq, A_DIM), lambda b, i: (b * steps + i, 0)),
        out_shape=jax.ShapeDtypeStruct((batch * seq, A_DIM), BF16),
        compiler_params=_params(2),
        name="na_attn",
    )(z_x, z_x, z_x, z_c, z_c, bias_tbl)


def _stack_group_queries(q, lo, hi):
    tiles = [q[:, j * LANES:(j + 1) * LANES] for j in range(B_Q_DIM // LANES)]
    return jnp.concatenate([t * lo for t in tiles] + [t * hi for t in tiles], axis=0)


def _sw_kernel(q_ref, k_ref, v_ref, kc_ref, vc_ref, sink_ref, o_ref, *, seq):
    lo, hi = _half_masks(BF16)
    n = pl.program_id(1)
    n_tiles = B_Q_DIM // LANES
    n_key = 3 * SW_BLOCK
    start = pl.multiple_of(jnp.clip((n - 1) * SW_BLOCK, 0, seq - n_key), SW_BLOCK)
    q6 = _stack_group_queries(q_ref[...], lo, hi)
    k = k_ref[pl.ds(start, n_key), :]
    v = v_ref[pl.ds(start, n_key), :]
    s = _dot_nt(q6, k)
    qpos = n * SW_BLOCK + lax.broadcasted_iota(jnp.int32, (SW_BLOCK, n_key), 0)
    kpos = start + lax.broadcasted_iota(jnp.int32, (SW_BLOCK, n_key), 1)
    valid = jnp.abs(qpos - kpos) <= SW_WINDOW
    s = jnp.where(valid[None], s.reshape(2 * n_tiles, SW_BLOCK, n_key), NEG_INF)
    s = s.reshape(2 * n_tiles * SW_BLOCK, n_key)
    s_ctx = _dot_nt(q6, kc_ref[...])
    o = _softmax_pv([s, s_ctx], [v, vc_ref[...]], extra_logit=sink_ref[...])
    for j, tile in enumerate(_merge_halves(o, n_tiles, SW_BLOCK)):
        o_ref[:, j * LANES:(j + 1) * LANES] = tile.astype(BF16)


def _sw(z_x, z_c, sink_col, *, batch, seq, n_ctx):
    nb = seq // SW_BLOCK
    return pl.pallas_call(
        functools.partial(_sw_kernel, seq=seq),
        grid=(batch, nb),
        in_specs=[pl.BlockSpec((SW_BLOCK, B_Q_DIM), lambda b, n: (b * nb + n, Z_QB // B_Q_DIM)),
                  pl.BlockSpec((seq, B_KV_DIM), lambda b, n: (b, Z_KB // B_KV_DIM)),
                  pl.BlockSpec((seq, B_KV_DIM), lambda b, n: (b, Z_VB // B_KV_DIM)),
                  pl.BlockSpec((n_ctx, B_KV_DIM), lambda b, n: (b, Z_KB // B_KV_DIM)),
                  pl.BlockSpec((n_ctx, B_KV_DIM), lambda b, n: (b, Z_VB // B_KV_DIM)),
                  _const_spec(sink_col.shape)],
        out_specs=pl.BlockSpec((SW_BLOCK, B_Q_DIM), lambda b, n: (b * nb + n, 0)),
        out_shape=jax.ShapeDtypeStruct((batch * seq, B_Q_DIM), BF16),
        compiler_params=_params(2),
        name="sw_attn",
    )(z_x, z_x, z_x, z_c, z_c, sink_col)


def _ctx_attn_kernel(qa_ref, ka_ref, va_ref, qb_ref, kb_ref, vb_ref, sink_ref, oa_ref, ob_ref):
    lo, hi = _half_masks(BF16)
    n_ctx = qa_ref.shape[0]
    for t in range(A_DIM // LANES):
        cols = slice(t * LANES, (t + 1) * LANES)
        qt = qa_ref[:, cols]
        q2 = jnp.concatenate([qt * lo, qt * hi], axis=0)
        o = _softmax_pv([_dot_nt(q2, ka_ref[:, cols])], [va_ref[:, cols]])
        oa_ref[:, cols] = _merge_halves(o, 1, n_ctx)[0].astype(BF16)
    n_tiles = B_Q_DIM // LANES
    q6 = _stack_group_queries(qb_ref[...], lo, hi)
    o = _softmax_pv([_dot_nt(q6, kb_ref[...])], [vb_ref[...]], extra_logit=sink_ref[...])
    for j, tile in enumerate(_merge_halves(o, n_tiles, n_ctx)):
        ob_ref[:, j * LANES:(j + 1) * LANES] = tile.astype(BF16)


def _ctx_attn(z_c, sink_col, *, batch, n_ctx):
    spec = lambda width, off: pl.BlockSpec((n_ctx, width), lambda b: (b, off // width))
    return pl.pallas_call(
        _ctx_attn_kernel,
        grid=(batch,),
        in_specs=[spec(A_DIM, Z_QA), spec(A_DIM, Z_KA), spec(A_DIM, Z_VA),
                  spec(B_Q_DIM, Z_QB), spec(B_KV_DIM, Z_KB), spec(B_KV_DIM, Z_VB),
                  _const_spec(sink_col.shape)],
        out_specs=[pl.BlockSpec((n_ctx, A_DIM), lambda b: (b, 0)),
                   pl.BlockSpec((n_ctx, B_Q_DIM), lambda b: (b, 0))],
        out_shape=[jax.ShapeDtypeStruct((batch * n_ctx, A_DIM), BF16),
                   jax.ShapeDtypeStruct((batch * n_ctx, B_Q_DIM), BF16)],
        compiler_params=_params(1),
        name="ctx_attn",
    )(z_c, z_c, z_c, z_c, z_c, z_c, sink_col)


def _conv_kernel(u_ref, w_ref, b_ref, g_ref, beta_ref, o_ref, hp_ref):
    length = u_ref.shape[0]
    hp_ref[0:CONV_HALO, :] = jnp.zeros((CONV_HALO, CONV_DIM), F32)
    hp_ref[CONV_HALO + length:CONV_HALO + length + CONV_HALO, :] = jnp.zeros((CONV_HALO, CONV_DIM), F32)

    def glu_body(c, carry):
        r0 = pl.multiple_of(c * CONV_CHUNK, CONV_CHUNK)
        u = u_ref[pl.ds(r0, CONV_CHUNK), :].astype(F32)
        hp_ref[pl.ds(CONV_HALO + r0, CONV_CHUNK), :] = u[:, :CONV_DIM] * _sigmoid(u[:, CONV_DIM:])
        return carry

    lax.fori_loop(0, length // CONV_CHUNK, glu_body, 0)

    def conv_body(c, carry):
        r0 = pl.multiple_of(c * CONV_CHUNK, CONV_CHUNK)
        win = hp_ref[pl.ds(r0, CONV_CHUNK + 2 * CONV_HALO), :]
        lead = CONV_HALO - CONV_WIDTH // 2
        span = CONV_CHUNK + 2 * CONV_HALO - SUBLANES
        acc = jnp.zeros((CONV_CHUNK, CONV_DIM), F32)
        for res in range(SUBLANES):
            shifted = win[res:res + span]
            for k in range(CONV_WIDTH):
                if (k + lead) % SUBLANES == res:
                    off = k + lead - res
                    acc = acc + shifted[off:off + CONV_CHUNK] * w_ref[k:k + 1, :]
        acc = acc + b_ref[...]
        mu = jnp.mean(acc, axis=-1, keepdims=True)
        cen = acc - mu
        var = jnp.mean(cen * cen, axis=-1, keepdims=True)
        y = cen * lax.rsqrt(var + EPS) * g_ref[...] + beta_ref[...]
        o_ref[pl.ds(r0, CONV_CHUNK), :] = _silu(y).astype(BF16)
        return carry

    lax.fori_loop(0, length // CONV_CHUNK, conv_body, 0)


def _conv(z, w, b, g, beta, *, n_seq, length):
    return pl.pallas_call(
        _conv_kernel,
        grid=(n_seq,),
        in_specs=[pl.BlockSpec((length, 2 * CONV_DIM), lambda s: (s, Z_U // (2 * CONV_DIM))),
                  _const_spec(w.shape), _const_spec(b.shape), _const_spec(g.shape), _const_spec(beta.shape)],
        out_specs=pl.BlockSpec((length, CONV_DIM), lambda s: (s, 0)),
        out_shape=jax.ShapeDtypeStruct((n_seq * length, CONV_DIM), BF16),
        scratch_shapes=[pltpu.VMEM((length + 2 * CONV_HALO, CONV_DIM), F32)],
        compiler_params=_params(1),
        name="conv",
    )(z, w, b, g, beta)


def _merge_kernel(x_ref, mod_ref, g_ref, ya_ref, yb_ref, yc_ref, wg_ref, bg_ref,
                  woa_ref, wob_ref, woc_ref, wo_ref, o_ref):
    x = x_ref[...]
    d = x.shape[1]
    h = _norm_modulate(x, g_ref[...], mod_ref[0, 3:4, :], mod_ref[0, 4:5, :]).astype(BF16)
    branches = ((ya_ref, woa_ref), (yb_ref, wob_ref), (yc_ref, woc_ref))
    mix = None
    for j, (y_ref, w_ref) in enumerate(branches):
        logits = _dot(h, wg_ref[:, j * d:(j + 1) * d]) + bg_ref[:, j * d:(j + 1) * d]
        term = _sigmoid(logits) * _dot(y_ref[...], w_ref[...])
        mix = term if mix is None else mix + term
    out = _dot(mix.astype(BF16), wo_ref[...])
    o_ref[...] = x + mod_ref[0, 5:6, :] * out


def _merge(x, mod, g, ya, yb, yc, w_gate, b_gate, w_oa, w_ob, w_oc, w_o, *, tiles_per_mod):
    t, d = x.shape
    tm = TOKEN_TILE
    tok = lambda width: pl.BlockSpec((tm, width), lambda i: (i, 0))
    return pl.pallas_call(
        _merge_kernel,
        grid=(t // tm,),
        in_specs=[tok(d),
                  pl.BlockSpec((1, N_MOD, d), lambda i: (i // tiles_per_mod, 0, 0)),
                  _const_spec((1, d)),
                  tok(A_DIM), tok(B_Q_DIM), tok(CONV_DIM),
                  _const_spec(w_gate.shape), _const_spec(b_gate.shape),
                  _const_spec(w_oa.shape), _const_spec(w_ob.shape), _const_spec(w_oc.shape),
                  _const_spec(w_o.shape)],
        out_specs=tok(d),
        out_shape=jax.ShapeDtypeStruct((t, d), F32),
        compiler_params=_params(1),
        name="merge",
    )(x, mod, g, ya, yb, yc, w_gate, b_gate, w_oa, w_ob, w_oc, w_o)


def _rope_tables(seq):
    t = jnp.arange(seq, dtype=jnp.int32)
    row = (t // GRID_W).astype(F32)
    col = (t % GRID_W).astype(F32)
    n_freq = HEAD_DIM // 4
    inv_freq = ROPE_BASE ** (-jnp.arange(n_freq, dtype=F32) / n_freq)
    ang = jnp.concatenate([row[:, None] * inv_freq, col[:, None] * inv_freq], axis=-1)
    cos, sin = jnp.cos(ang), jnp.sin(ang)
    reps = LANES // (HEAD_DIM // 2)
    cos_tbl = jnp.tile(cos, (1, reps))
    sin_tbl = jnp.tile(jnp.concatenate([-sin, sin], axis=-1), (1, reps // 2))
    return cos_tbl, sin_tbl


def _na_bias_table(rpb):
    d = np.arange(NA_WIN_H)
    y = np.arange(NA_WIN_H)
    qc = np.arange(GRID_W)
    kc = np.arange(GRID_W)
    dyi = y[None, :] - d[:, None] + (NA_WIN_H - 1)
    dxi = np.clip(kc[None, :] - qc[:, None] + (NA_WIN_W - 1), 0, 2 * NA_WIN_W - 2)
    win_c0 = np.clip(qc - NA_WIN_W // 2, 0, GRID_W - NA_WIN_W)[:, None]
    in_win = (kc[None, :] >= win_c0) & (kc[None, :] < win_c0 + NA_WIN_W)
    tbl = rpb[:, dyi[:, None, :, None], dxi[None, :, None, :]]
    tbl = jnp.where(in_win[None, None, :, None, :], tbl.astype(F32), NEG_INF)
    return tbl.reshape(rpb.shape[0], NA_WIN_H, GRID_W, NA_WIN_H * GRID_W)


_SW_HEAD_ORDER = (0, 3, 1, 4, 2, 5)


def _sw_head_perm():
    return np.concatenate([np.arange(h * HEAD_DIM, (h + 1) * HEAD_DIM) for h in _SW_HEAD_ORDER])


def _layer_weights(l, w_in, w_out_b):
    a, bq, bkv, cv = A_DIM, B_Q_DIM, B_KV_DIM, 2 * CONV_DIM
    scale = HEAD_DIM ** -0.5
    w = w_in[l]
    o_qb = 3 * a
    o_kb = o_qb + bq
    o_vb = o_kb + bkv
    o_u = o_vb + bkv
    o_gate = o_u + cv
    perm = _sw_head_perm()
    w_main = jnp.concatenate([
        w[:, 0:a] * scale, w[:, a:3 * a],
        w[:, o_qb:o_kb][:, perm] * scale,
        w[:, o_u:o_gate], w[:, o_kb:o_vb], w[:, o_vb:o_u]], axis=1).astype(BF16)
    w_gate = w[:, o_gate:].astype(BF16)
    w_ob = w_out_b[l][perm, :].astype(BF16)
    return w_main, w_gate, w_ob


def kernel(x, c, ctx, c_ctx, w_ada, b_ada, norm_g, w_ffn1_gu, w_ffn1_down, w_ffn2_gu, w_ffn2_down, w_in, b_gate, na_rpb, sw_sink, conv_dw_w, conv_dw_b, conv_ln_g, conv_ln_b, w_out_a, w_out_b, w_out_c, w_out, final_g):
    batch, seq, d = x.shape
    n_ctx = ctx.shape[1]
    depth = w_ada.shape[0]
    tiles_x = seq // TOKEN_TILE
    tiles_c = (batch * n_ctx) // TOKEN_TILE

    pad_rows = (-(batch + 1)) % 8
    cvec = jnp.concatenate([c, c_ctx[None, :], jnp.zeros((pad_rows, d), F32)], axis=0)
    mods = _ada(cvec, w_ada, b_ada).reshape(depth, batch + 1 + pad_rows, N_MOD, d)

    cos_tbl, sin_tbl = _rope_tables(seq)
    cos_one = jnp.ones((TOKEN_TILE, LANES), F32)
    sin_zero = jnp.zeros((TOKEN_TILE, LANES), F32)
    final_row = final_g.reshape(1, d)

    xs = x.reshape(batch * seq, d)
    hc = ctx.reshape(batch * n_ctx, d)
    for l in range(depth):
        last = l == depth - 1
        mod_x = mods[l, :batch]
        mod_c = mods[l, batch:batch + 1]
        g = norm_g[l]
        w_main, w_gate, w_ob = _layer_weights(l, w_in, w_out_b)
        ffn1 = (w_ffn1_gu[l].astype(BF16), w_ffn1_down[l].astype(BF16))
        ffn2 = (w_ffn2_gu[l].astype(BF16), w_ffn2_down[l].astype(BF16))
        merge_w = (w_gate, b_gate[l].reshape(1, GATE_DIM), w_out_a[l].astype(BF16), w_ob,
                   w_out_c[l].astype(BF16), w_out[l].astype(BF16))
        conv_w = (conv_dw_w[l], conv_dw_b[l].reshape(1, CONV_DIM), conv_ln_g[l].reshape(1, CONV_DIM),
                  conv_ln_b[l].reshape(1, CONV_DIM))
        bias_tbl = _na_bias_table(na_rpb[l])
        sink_col = jnp.repeat(sw_sink[l].astype(F32), SW_BLOCK).reshape(SW_HEADS * SW_BLOCK, 1)
        sink_col_c = jnp.repeat(sw_sink[l].astype(F32), n_ctx).reshape(SW_HEADS * n_ctx, 1)

        xs = _ffn(xs, mod_x, g[0:1], *ffn1, final_row, mod_off=0, final=False, tiles_per_mod=tiles_x)
        hc = _ffn(hc, mod_c, g[0:1], *ffn1, final_row, mod_off=0, final=False, tiles_per_mod=tiles_c)

        z_x = _inproj(xs, mod_x, g[1:2], w_main, cos_tbl, sin_tbl, tiles_per_mod=tiles_x, table_tiles=tiles_x)
        z_c = _inproj(hc, mod_c, g[1:2], w_main, cos_one, sin_zero, tiles_per_mod=tiles_c, table_tiles=1)

        ya = _na(z_x, z_c, bias_tbl, batch=batch, seq=seq, n_ctx=n_ctx)
        yb = _sw(z_x, z_c, sink_col, batch=batch, seq=seq, n_ctx=n_ctx)
        yc = _conv(z_x, *conv_w, n_seq=batch, length=seq)
        xs = _merge(xs, mod_x, g[1:2], ya, yb, yc, *merge_w, tiles_per_mod=tiles_x)
        if not last:
            ya_c, yb_c = _ctx_attn(z_c, sink_col_c, batch=batch, n_ctx=n_ctx)
            yc_c = _conv(z_c, *conv_w, n_seq=batch, length=n_ctx)
            hc = _merge(hc, mod_c, g[1:2], ya_c, yb_c, yc_c, *merge_w, tiles_per_mod=tiles_c)

        xs = _ffn(xs, mod_x, g[2:3], *ffn2, final_row, mod_off=6, final=last, tiles_per_mod=tiles_x)
        if not last:
            hc = _ffn(hc, mod_c, g[2:3], *ffn2, final_row, mod_off=6, final=False, tiles_per_mod=tiles_c)
    return xs.reshape(batch, seq, d)
```

```python
import functools
import math

import jax
import jax.numpy as jnp
import numpy as np
from jax import lax
from jax.experimental import pallas as pl
from jax.experimental.pallas import tpu as pltpu

D_MODEL = 1024
GRID_W = 64
HEAD_DIM = 64
NA_HEADS = 6
NA_WIN_H = 8
NA_WIN_W = 16
SW_HEADS = 6
SW_KV_HEADS = 2
SW_WINDOW = 128
SW_BLOCK = 128
CONV_DIM = 256
CONV_WIDTH = 31
D_FF = 2816
ROPE_BASE = 10000.0
EPS = 1e-6
NEG_INF = -1e30
N_MOD = 9
LOG2E = math.log2(math.e)

A_DIM = NA_HEADS * HEAD_DIM
B_Q_DIM = SW_HEADS * HEAD_DIM
B_KV_DIM = SW_KV_HEADS * HEAD_DIM
GATE_DIM = 3 * D_MODEL

LANES = 128
SUBLANES = 8
VMEM_BYTES_V7X = 64 * 1024 * 1024
VMEM_LIMIT = VMEM_BYTES_V7X - 8 * 1024 * 1024

P_QA, P_KA, P_QB = 0, A_DIM, 2 * A_DIM
P_VA = 2 * A_DIM + B_Q_DIM
P_VB = P_VA + A_DIM
P_U = P_VB + B_KV_DIM
P_KB = P_U + 2 * CONV_DIM
P_DIM = P_KB + B_KV_DIM
ZA_DIM = 2 * A_DIM + B_Q_DIM
VAUG = 2 * LANES
N_VA_TILES = A_DIM // LANES
ZV_DIM = (N_VA_TILES + 1) * VAUG
ZU_DIM = 2 * CONV_DIM + B_KV_DIM

TOKEN_TILE = 512
FF_CHUNKS = ((0, 768), (768, 1536), (1536, 2304), (2304, 2816))
NA_ROWS_PER_STEP = 4
SW_BLOCKS_PER_STEP = 4
CONV_CHUNK = 128
CONV_HALO = 16

BF16 = jnp.bfloat16
F32 = jnp.float32


def _params(n_axes):
    return pltpu.CompilerParams(dimension_semantics=("arbitrary",) * n_axes,
                                vmem_limit_bytes=VMEM_LIMIT)


def _const_spec(shape):
    nd = len(shape)
    return pl.BlockSpec(shape, lambda *_: (0,) * nd, pipeline_mode=pl.Buffered(1))


def _sigmoid(v):
    return 1.0 / (1.0 + jnp.exp(-v))


def _silu(v):
    return v * _sigmoid(v)


def _norm_modulate(x, g, shift, scale):
    ms = jnp.mean(x * x, axis=-1, keepdims=True)
    y = x * lax.rsqrt(ms + EPS) * g
    return y * (1.0 + scale) + shift


def _dot(a, b):
    return jnp.dot(a, b, preferred_element_type=F32)


def _dot_nt(a, b):
    return lax.dot_general(a, b, (((1,), (1,)), ((), ())), preferred_element_type=F32)


def _ada_kernel(c_ref, w_ref, b_ref, o_ref):
    s = _silu(c_ref[...]).astype(BF16)
    o_ref[0] = _dot(s, w_ref[0].astype(BF16)) + b_ref[0]


def _ada(cvec, w_ada, b_ada):
    depth, d, n = w_ada.shape
    rows = cvec.shape[0]
    bn = 1024
    return pl.pallas_call(
        _ada_kernel,
        grid=(depth, n // bn),
        in_specs=[pl.BlockSpec((rows, d), lambda l, j: (0, 0)),
                  pl.BlockSpec((1, d, bn), lambda l, j: (l, 0, j)),
                  pl.BlockSpec((1, 1, bn), lambda l, j: (l, 0, j))],
        out_specs=pl.BlockSpec((1, rows, bn), lambda l, j: (l, 0, j)),
        out_shape=jax.ShapeDtypeStruct((depth, rows, n), F32),
        compiler_params=_params(2),
        name="ada",
    )(cvec, w_ada, b_ada.reshape(depth, 1, n))


def _ffn_kernel(x_ref, mod_ref, g_ref, wgu_ref, wd_ref, fg_ref, o_ref, *, mod_off, final):
    x = x_ref[...]
    shift = mod_ref[0, mod_off:mod_off + 1, :]
    scale = mod_ref[0, mod_off + 1:mod_off + 2, :]
    gate = mod_ref[0, mod_off + 2:mod_off + 3, :]
    h = _norm_modulate(x, g_ref[...], shift, scale).astype(BF16)
    acc = jnp.zeros(x.shape, F32)
    for c0, c1 in FF_CHUNKS:
        a = _dot(h, wgu_ref[:, c0:c1])
        b = _dot(h, wgu_ref[:, D_FF + c0:D_FF + c1])
        act = (_silu(a) * b).astype(BF16)
        acc = acc + _dot(act, wd_ref[c0:c1, :])
    y = x + (0.5 * gate) * acc
    if final:
        ms = jnp.mean(y * y, axis=-1, keepdims=True)
        y = y * lax.rsqrt(ms + EPS) * fg_ref[...]
    o_ref[...] = y


def _ffn(x, mod, g, w_gu, w_down, final_g, *, mod_off, final, tiles_per_mod):
    t, d = x.shape
    tm = TOKEN_TILE
    return pl.pallas_call(
        functools.partial(_ffn_kernel, mod_off=mod_off, final=final),
        grid=(t // tm,),
        in_specs=[pl.BlockSpec((tm, d), lambda i: (i, 0)),
                  pl.BlockSpec((1, N_MOD, d), lambda i: (i // tiles_per_mod, 0, 0)),
                  _const_spec((1, d)),
                  _const_spec(w_gu.shape),
                  _const_spec(w_down.shape),
                  _const_spec((1, d))],
        out_specs=pl.BlockSpec((tm, d), lambda i: (i, 0)),
        out_shape=jax.ShapeDtypeStruct((t, d), F32),
        compiler_params=_params(1),
        name="ffn",
    )(x, mod, g, w_gu, w_down, final_g)


def _rope_tile(t, cos, sin_signed, first_half):
    fwd = pltpu.roll(t, LANES - HEAD_DIM // 2, axis=1)
    bwd = pltpu.roll(t, HEAD_DIM // 2, axis=1)
    return t * cos + jnp.where(first_half, fwd, bwd) * sin_signed


def _inproj_kernel(x_ref, mod_ref, g_ref, w_ref, cos_ref, sin_ref, za_ref, zv_ref, zu_ref):
    x = x_ref[...]
    h = _norm_modulate(x, g_ref[...], mod_ref[0, 3:4, :], mod_ref[0, 4:5, :]).astype(BF16)
    z = _dot(h, w_ref[...])
    cos = cos_ref[...]
    sin = sin_ref[...]
    lane = lax.broadcasted_iota(jnp.int32, (1, LANES), 1)
    first_half = (lane % HEAD_DIM) < (HEAD_DIM // 2)
    ones = jnp.ones((x.shape[0], LANES), BF16)

    za_ref[:, 0:P_QB] = z[:, 0:P_QB].astype(BF16)
    for c0 in range(0, B_Q_DIM, LANES):
        za_ref[:, P_QB + c0:P_QB + c0 + LANES] = _rope_tile(
            z[:, P_QB + c0:P_QB + c0 + LANES], cos, sin, first_half).astype(BF16)
    for j in range(N_VA_TILES + 1):
        zv_ref[:, j * VAUG:j * VAUG + LANES] = z[:, P_VA + j * LANES:P_VA + (j + 1) * LANES].astype(BF16)
        zv_ref[:, j * VAUG + LANES:(j + 1) * VAUG] = ones
    zu_ref[:, 0:2 * CONV_DIM] = z[:, P_U:P_KB].astype(BF16)
    zu_ref[:, 2 * CONV_DIM:ZU_DIM] = _rope_tile(z[:, P_KB:P_DIM], cos, sin, first_half).astype(BF16)


def _inproj(x, mod, g, w_main, cos_tbl, sin_tbl, *, tiles_per_mod, table_tiles):
    t, d = x.shape
    tm = TOKEN_TILE
    tok = lambda width: pl.BlockSpec((tm, width), lambda i: (i, 0))
    return pl.pallas_call(
        _inproj_kernel,
        grid=(t // tm,),
        in_specs=[tok(d),
                  pl.BlockSpec((1, N_MOD, d), lambda i: (i // tiles_per_mod, 0, 0)),
                  _const_spec((1, d)),
                  _const_spec(w_main.shape),
                  pl.BlockSpec((tm, LANES), lambda i: (i % table_tiles, 0)),
                  pl.BlockSpec((tm, LANES), lambda i: (i % table_tiles, 0))],
        out_specs=[tok(ZA_DIM), tok(ZV_DIM), tok(ZU_DIM)],
        out_shape=[jax.ShapeDtypeStruct((t, ZA_DIM), BF16),
                   jax.ShapeDtypeStruct((t, ZV_DIM), BF16),
                   jax.ShapeDtypeStruct((t, ZU_DIM), BF16)],
        compiler_params=_params(1),
        name="inproj",
    )(x, mod, g, w_main, cos_tbl, sin_tbl)


def _half_masks(dtype):
    lane = lax.broadcasted_iota(jnp.int32, (1, LANES), 1)
    lo = (lane < HEAD_DIM).astype(F32).astype(dtype)
    hi = (lane >= HEAD_DIM).astype(F32).astype(dtype)
    return lo, hi


def _lane_tiles(s):
    return [s[:, c:c + LANES] for c in range(0, s.shape[1], LANES)]


def _row_max(s_list):
    tiles = [t for s in s_list for t in _lane_tiles(s)]
    return functools.reduce(jnp.maximum, tiles).max(axis=-1, keepdims=True)


def _exp2_bf16(s, m):
    return jnp.concatenate([jnp.exp2(t - m).astype(BF16) for t in _lane_tiles(s)], axis=1)


def _merge_halves(o, n_tiles, rows):
    lane = lax.broadcasted_iota(jnp.int32, (1, LANES), 1)
    low = lane < HEAD_DIM
    return [jnp.where(low, o[j * rows:(j + 1) * rows], o[(n_tiles + j) * rows:(n_tiles + j + 1) * rows])
            for j in range(n_tiles)]


def _stack_pair(qt, lo, hi):
    return jnp.concatenate([qt * lo, qt * hi], axis=0)


def _stack_group_queries(q, lo, hi):
    tiles = _lane_tiles(q)
    return jnp.concatenate([t * lo for t in tiles] + [t * hi for t in tiles], axis=0)


def _na_kernel(q_ref, k_ref, v_ref, kc_ref, vc_ref, bias_ref, o_ref, *, rows_total):
    lo, hi = _half_masks(BF16)
    step = pl.program_id(1)
    n_key = NA_WIN_H * GRID_W
    pair_rows = 2 * GRID_W
    for t in range(N_VA_TILES):
        cols = slice(t * LANES, (t + 1) * LANES)
        vcols = slice(t * VAUG, (t + 1) * VAUG)
        q2 = [_stack_pair(q_ref[rr * GRID_W:(rr + 1) * GRID_W, cols], lo, hi)
              for rr in range(NA_ROWS_PER_STEP)]
        s_ctx_all = _dot_nt(jnp.concatenate(q2, axis=0), kc_ref[:, cols])
        pv_loc, p_ctx = [], []
        for rr in range(NA_ROWS_PER_STEP):
            r = step * NA_ROWS_PER_STEP + rr
            r0 = jnp.clip(r - NA_WIN_H // 2, 0, rows_total - NA_WIN_H)
            d = r - r0
            ks = pl.multiple_of(r0 * GRID_W, GRID_W)
            bias = jnp.concatenate([bias_ref[2 * t, d], bias_ref[2 * t + 1, d]], axis=0)
            s_loc = _dot_nt(q2[rr], k_ref[pl.ds(ks, n_key), cols]) + bias
            s_ctx = s_ctx_all[rr * pair_rows:(rr + 1) * pair_rows]
            m = _row_max([s_loc, s_ctx])
            pv_loc.append(_dot(_exp2_bf16(s_loc, m), v_ref[pl.ds(ks, n_key), vcols]))
            p_ctx.append(_exp2_bf16(s_ctx, m))
        acc = jnp.concatenate(pv_loc, axis=0) + _dot(jnp.concatenate(p_ctx, axis=0), vc_ref[:, vcols])
        o = acc[:, :LANES] / acc[:, LANES:]
        for rr in range(NA_ROWS_PER_STEP):
            tile = _merge_halves(o[rr * pair_rows:(rr + 1) * pair_rows], 1, GRID_W)[0]
            o_ref[rr * GRID_W:(rr + 1) * GRID_W, cols] = tile.astype(BF16)


def _na(za_x, zv_x, za_c, zv_c, bias_tbl, *, batch, seq, n_ctx):
    rows_total = seq // GRID_W
    steps = rows_total // NA_ROWS_PER_STEP
    tq = NA_ROWS_PER_STEP * GRID_W
    va_w = N_VA_TILES * VAUG
    return pl.pallas_call(
        functools.partial(_na_kernel, rows_total=rows_total),
        grid=(batch, steps),
        in_specs=[pl.BlockSpec((tq, A_DIM), lambda b, i: (b * steps + i, P_QA // A_DIM)),
                  pl.BlockSpec((seq, A_DIM), lambda b, i: (b, P_KA // A_DIM)),
                  pl.BlockSpec((seq, va_w), lambda b, i: (b, 0)),
                  pl.BlockSpec((n_ctx, A_DIM), lambda b, i: (b, P_KA // A_DIM)),
                  pl.BlockSpec((n_ctx, va_w), lambda b, i: (b, 0)),
                  _const_spec(bias_tbl.shape)],
        out_specs=pl.BlockSpec((tq, A_DIM), lambda b, i: (b * steps + i, 0)),
        out_shape=jax.ShapeDtypeStruct((batch * seq, A_DIM), BF16),
        compiler_params=_params(2),
        name="na_attn",
    )(za_x, za_x, zv_x, za_c, zv_c, bias_tbl)


def _sw_kernel(q_ref, k_ref, v_ref, kc_ref, vc_ref, sink_ref, o_ref, *, seq):
    lo, hi = _half_masks(BF16)
    step = pl.program_id(1)
    n_tiles = B_Q_DIM // LANES
    n_key = 3 * SW_BLOCK
    grp = SW_HEADS * SW_BLOCK
    sink = sink_ref[...]
    q6 = [_stack_group_queries(q_ref[j * SW_BLOCK:(j + 1) * SW_BLOCK, :], lo, hi)
          for j in range(SW_BLOCKS_PER_STEP)]
    s_ctx_all = _dot_nt(jnp.concatenate(q6, axis=0), kc_ref[...])
    pv_loc, p_ctx, p_sink = [], [], []
    for j in range(SW_BLOCKS_PER_STEP):
        n = step * SW_BLOCKS_PER_STEP + j
        start = pl.multiple_of(jnp.clip((n - 1) * SW_BLOCK, 0, seq - n_key), SW_BLOCK)
        s = _dot_nt(q6[j], k_ref[pl.ds(start, n_key), :])
        qpos = n * SW_BLOCK + lax.broadcasted_iota(jnp.int32, (SW_BLOCK, n_key), 0)
        kpos = start + lax.broadcasted_iota(jnp.int32, (SW_BLOCK, n_key), 1)
        valid = jnp.abs(qpos - kpos) <= SW_WINDOW
        s = jnp.where(valid[None], s.reshape(SW_HEADS, SW_BLOCK, n_key), NEG_INF).reshape(grp, n_key)
        s_ctx = s_ctx_all[j * grp:(j + 1) * grp]
        m = jnp.maximum(_row_max([s, s_ctx]), sink)
        pv_loc.append(_dot(_exp2_bf16(s, m), v_ref[pl.ds(start, n_key), :]))
        p_ctx.append(_exp2_bf16(s_ctx, m))
        p_sink.append(jnp.exp2(sink - m))
    acc = jnp.concatenate(pv_loc, axis=0) + _dot(jnp.concatenate(p_ctx, axis=0), vc_ref[...])
    o = acc[:, :LANES] / (acc[:, LANES:] + jnp.concatenate(p_sink, axis=0))
    for j in range(SW_BLOCKS_PER_STEP):
        for t, tile in enumerate(_merge_halves(o[j * grp:(j + 1) * grp], n_tiles, SW_BLOCK)):
            o_ref[j * SW_BLOCK:(j + 1) * SW_BLOCK, t * LANES:(t + 1) * LANES] = tile.astype(BF16)


def _sw(za_x, zv_x, zu_x, za_c, zv_c, zu_c, sink_rep, *, batch, seq, n_ctx):
    tq = SW_BLOCK * SW_BLOCKS_PER_STEP
    nb = seq // tq
    kb_blk = (2 * CONV_DIM) // B_KV_DIM
    return pl.pallas_call(
        functools.partial(_sw_kernel, seq=seq),
        grid=(batch, nb),
        in_specs=[pl.BlockSpec((tq, B_Q_DIM), lambda b, n: (b * nb + n, P_QB // B_Q_DIM)),
                  pl.BlockSpec((seq, B_KV_DIM), lambda b, n: (b, kb_blk)),
                  pl.BlockSpec((seq, VAUG), lambda b, n: (b, N_VA_TILES)),
                  pl.BlockSpec((n_ctx, B_KV_DIM), lambda b, n: (b, kb_blk)),
                  pl.BlockSpec((n_ctx, VAUG), lambda b, n: (b, N_VA_TILES)),
                  _const_spec(sink_rep.shape)],
        out_specs=pl.BlockSpec((tq, B_Q_DIM), lambda b, n: (b * nb + n, 0)),
        out_shape=jax.ShapeDtypeStruct((batch * seq, B_Q_DIM), BF16),
        compiler_params=_params(2),
        name="sw_attn",
    )(za_x, zu_x, zv_x, zu_c, zv_c, sink_rep)


def _ctx_attn_kernel(qa_ref, ka_ref, va_ref, qb_ref, kb_ref, vb_ref, sink_ref, oa_ref, ob_ref):
    lo, hi = _half_masks(BF16)
    n_ctx = qa_ref.shape[0]
    for t in range(N_VA_TILES):
        cols = slice(t * LANES, (t + 1) * LANES)
        s = _dot_nt(_stack_pair(qa_ref[:, cols], lo, hi), ka_ref[:, cols])
        acc = _dot(_exp2_bf16(s, _row_max([s])), va_ref[:, t * VAUG:(t + 1) * VAUG])
        o = acc[:, :LANES] / acc[:, LANES:]
        oa_ref[:, cols] = _merge_halves(o, 1, n_ctx)[0].astype(BF16)
    sink = sink_ref[...]
    s = _dot_nt(_stack_group_queries(qb_ref[...], lo, hi), kb_ref[...])
    m = jnp.maximum(_row_max([s]), sink)
    acc = _dot(_exp2_bf16(s, m), vb_ref[...])
    o = acc[:, :LANES] / (acc[:, LANES:] + jnp.exp2(sink - m))
    for t, tile in enumerate(_merge_halves(o, B_Q_DIM // LANES, n_ctx)):
        ob_ref[:, t * LANES:(t + 1) * LANES] = tile.astype(BF16)


def _ctx_attn(za_c, zv_c, zu_c, sink_rep, *, batch, n_ctx):
    spec = lambda width, blk: pl.BlockSpec((n_ctx, width), lambda b: (b, blk))
    return pl.pallas_call(
        _ctx_attn_kernel,
        grid=(batch,),
        in_specs=[spec(A_DIM, P_QA // A_DIM), spec(A_DIM, P_KA // A_DIM), spec(N_VA_TILES * VAUG, 0),
                  spec(B_Q_DIM, P_QB // B_Q_DIM), spec(B_KV_DIM, (2 * CONV_DIM) // B_KV_DIM),
                  spec(VAUG, N_VA_TILES),
                  _const_spec(sink_rep.shape)],
        out_specs=[pl.BlockSpec((n_ctx, A_DIM), lambda b: (b, 0)),
                   pl.BlockSpec((n_ctx, B_Q_DIM), lambda b: (b, 0))],
        out_shape=[jax.ShapeDtypeStruct((batch * n_ctx, A_DIM), BF16),
                   jax.ShapeDtypeStruct((batch * n_ctx, B_Q_DIM), BF16)],
        compiler_params=_params(1),
        name="ctx_attn",
    )(za_c, za_c, zv_c, za_c, zu_c, zv_c, sink_rep)


def _conv_kernel(u_ref, w_ref, b_ref, g_ref, beta_ref, o_ref, hp_ref):
    length = u_ref.shape[0]
    hp_ref[0:CONV_HALO, :] = jnp.zeros((CONV_HALO, CONV_DIM), F32)
    hp_ref[CONV_HALO + length:CONV_HALO + length + CONV_HALO, :] = jnp.zeros((CONV_HALO, CONV_DIM), F32)

    def glu_body(c, carry):
        r0 = pl.multiple_of(c * CONV_CHUNK, CONV_CHUNK)
        u = u_ref[pl.ds(r0, CONV_CHUNK), :].astype(F32)
        hp_ref[pl.ds(CONV_HALO + r0, CONV_CHUNK), :] = u[:, :CONV_DIM] * _sigmoid(u[:, CONV_DIM:])
        return carry

    lax.fori_loop(0, length // CONV_CHUNK, glu_body, 0)

    def conv_body(c, carry):
        r0 = pl.multiple_of(c * CONV_CHUNK, CONV_CHUNK)
        win = hp_ref[pl.ds(r0, CONV_CHUNK + 2 * CONV_HALO), :]
        lead = CONV_HALO - CONV_WIDTH // 2
        span = CONV_CHUNK + 2 * CONV_HALO - SUBLANES
        acc = jnp.zeros((CONV_CHUNK, CONV_DIM), F32)
        for res in range(SUBLANES):
            shifted = win[res:res + span]
            for k in range(CONV_WIDTH):
                if (k + lead) % SUBLANES == res:
                    off = k + lead - res
                    acc = acc + shifted[off:off + CONV_CHUNK] * w_ref[k:k + 1, :]
        acc = acc + b_ref[...]
        mu = jnp.mean(acc, axis=-1, keepdims=True)
        cen = acc - mu
        var = jnp.mean(cen * cen, axis=-1, keepdims=True)
        y = cen * lax.rsqrt(var + EPS) * g_ref[...] + beta_ref[...]
        o_ref[pl.ds(r0, CONV_CHUNK), :] = _silu(y).astype(BF16)
        return carry

    lax.fori_loop(0, length // CONV_CHUNK, conv_body, 0)


def _conv(zu, w, b, g, beta, *, n_seq, length):
    return pl.pallas_call(
        _conv_kernel,
        grid=(n_seq,),
        in_specs=[pl.BlockSpec((length, 2 * CONV_DIM), lambda s: (s, 0)),
                  _const_spec(w.shape), _const_spec(b.shape), _const_spec(g.shape), _const_spec(beta.shape)],
        out_specs=pl.BlockSpec((length, CONV_DIM), lambda s: (s, 0)),
        out_shape=jax.ShapeDtypeStruct((n_seq * length, CONV_DIM), BF16),
        scratch_shapes=[pltpu.VMEM((length + 2 * CONV_HALO, CONV_DIM), F32)],
        compiler_params=_params(1),
        name="conv",
    )(zu, w, b, g, beta)


def _merge_kernel(x_ref, mod_ref, g_ref, ya_ref, yb_ref, yc_ref, wg_ref, bg_ref,
                  woa_ref, wob_ref, woc_ref, wo_ref, o_ref):
    x = x_ref[...]
    d = x.shape[1]
    h = _norm_modulate(x, g_ref[...], mod_ref[0, 3:4, :], mod_ref[0, 4:5, :]).astype(BF16)
    branches = ((ya_ref, woa_ref), (yb_ref, wob_ref), (yc_ref, woc_ref))
    mix = None
    for j, (y_ref, w_ref) in enumerate(branches):
        logits = _dot(h, wg_ref[:, j * d:(j + 1) * d]) + bg_ref[:, j * d:(j + 1) * d]
        term = _sigmoid(logits) * _dot(y_ref[...], w_ref[...])
        mix = term if mix is None else mix + term
    out = _dot(mix.astype(BF16), wo_ref[...])
    o_ref[...] = x + mod_ref[0, 5:6, :] * out


def _merge(x, mod, g, ya, yb, yc, w_gate, b_gate, w_oa, w_ob, w_oc, w_o, *, tiles_per_mod):
    t, d = x.shape
    tm = TOKEN_TILE
    tok = lambda width: pl.BlockSpec((tm, width), lambda i: (i, 0))
    return pl.pallas_call(
        _merge_kernel,
        grid=(t // tm,),
        in_specs=[tok(d),
                  pl.BlockSpec((1, N_MOD, d), lambda i: (i // tiles_per_mod, 0, 0)),
                  _const_spec((1, d)),
                  tok(A_DIM), tok(B_Q_DIM), tok(CONV_DIM),
                  _const_spec(w_gate.shape), _const_spec(b_gate.shape),
                  _const_spec(w_oa.shape), _const_spec(w_ob.shape), _const_spec(w_oc.shape),
                  _const_spec(w_o.shape)],
        out_specs=tok(d),
        out_shape=jax.ShapeDtypeStruct((t, d), F32),
        compiler_params=_params(1),
        name="merge",
    )(x, mod, g, ya, yb, yc, w_gate, b_gate, w_oa, w_ob, w_oc, w_o)


def _rope_tables(seq):
    t = jnp.arange(seq, dtype=jnp.int32)
    row = (t // GRID_W).astype(F32)
    col = (t % GRID_W).astype(F32)
    n_freq = HEAD_DIM // 4
    inv_freq = ROPE_BASE ** (-jnp.arange(n_freq, dtype=F32) / n_freq)
    ang = jnp.concatenate([row[:, None] * inv_freq, col[:, None] * inv_freq], axis=-1)
    cos, sin = jnp.cos(ang), jnp.sin(ang)
    reps = LANES // (HEAD_DIM // 2)
    cos_tbl = jnp.tile(cos, (1, reps))
    sin_tbl = jnp.tile(jnp.concatenate([-sin, sin], axis=-1), (1, reps // 2))
    return cos_tbl, sin_tbl


def _na_bias_table(rpb):
    heads, n_dy, _ = rpb.shape
    n = GRID_W
    qc = np.arange(n)
    kc = np.arange(n)
    win_c0 = np.clip(qc - NA_WIN_W // 2, 0, n - NA_WIN_W)[:, None]
    in_win = (kc[None, :] >= win_c0) & (kc[None, :] < win_c0 + NA_WIN_W)
    pad = (n - 1) - (NA_WIN_W - 1)
    p = jnp.pad(rpb.astype(F32) * LOG2E, ((0, 0), (0, 0), (pad, pad + 1)))
    flat = jnp.tile(p, (1, 1, n))[:, :, :n * (2 * n - 1)]
    toe = flat.reshape(heads, n_dy, n, 2 * n - 1)[:, :, :, n - 1:]
    toe = jnp.where(in_win[None, None], toe, NEG_INF)
    per_d = [toe[:, NA_WIN_H - 1 - d:2 * NA_WIN_H - 1 - d].transpose(0, 2, 1, 3)
             .reshape(heads, n, NA_WIN_H * n) for d in range(NA_WIN_H)]
    return jnp.stack(per_d, axis=1)


_SW_HEAD_ORDER = (0, 3, 1, 4, 2, 5)


def _sw_head_perm():
    return np.concatenate([np.arange(h * HEAD_DIM, (h + 1) * HEAD_DIM) for h in _SW_HEAD_ORDER])


def _layer_weights(l, w_in, w_out_b):
    a, bq, bkv, cv = A_DIM, B_Q_DIM, B_KV_DIM, 2 * CONV_DIM
    q_scale = HEAD_DIM ** -0.5 * LOG2E
    w = w_in[l]
    o_qb = 3 * a
    o_kb = o_qb + bq
    o_vb = o_kb + bkv
    o_u = o_vb + bkv
    o_gate = o_u + cv
    perm = _sw_head_perm()
    w_main = jnp.concatenate([
        w[:, 0:a] * q_scale, w[:, a:2 * a],
        w[:, o_qb:o_kb][:, perm] * q_scale,
        w[:, 2 * a:3 * a], w[:, o_vb:o_u],
        w[:, o_u:o_gate], w[:, o_kb:o_vb]], axis=1).astype(BF16)
    w_gate = w[:, o_gate:].astype(BF16)
    w_ob = w_out_b[l][perm, :].astype(BF16)
    return w_main, w_gate, w_ob


def _sink_rows(sink, rows_per_head):
    col = jnp.repeat(sink.astype(F32) * LOG2E, rows_per_head)
    return jnp.broadcast_to(col[:, None], (col.shape[0], LANES))


def kernel(x, c, ctx, c_ctx, w_ada, b_ada, norm_g, w_ffn1_gu, w_ffn1_down, w_ffn2_gu, w_ffn2_down, w_in, b_gate, na_rpb, sw_sink, conv_dw_w, conv_dw_b, conv_ln_g, conv_ln_b, w_out_a, w_out_b, w_out_c, w_out, final_g):
    batch, seq, d = x.shape
    n_ctx = ctx.shape[1]
    depth = w_ada.shape[0]
    tiles_x = seq // TOKEN_TILE
    tiles_c = (batch * n_ctx) // TOKEN_TILE

    pad_rows = (-(batch + 1)) % SUBLANES
    cvec = jnp.concatenate([c, c_ctx[None, :], jnp.zeros((pad_rows, d), F32)], axis=0)
    mods = _ada(cvec, w_ada, b_ada).reshape(depth, batch + 1 + pad_rows, N_MOD, d)

    cos_tbl, sin_tbl = _rope_tables(seq)
    cos_one = jnp.ones((TOKEN_TILE, LANES), F32)
    sin_zero = jnp.zeros((TOKEN_TILE, LANES), F32)
    final_row = final_g.reshape(1, d)

    xs = x.reshape(batch * seq, d)
    hc = ctx.reshape(batch * n_ctx, d)
    for l in range(depth):
        last = l == depth - 1
        mod_x = mods[l, :batch]
        mod_c = mods[l, batch:batch + 1]
        g = norm_g[l]
        w_main, w_gate, w_ob = _layer_weights(l, w_in, w_out_b)
        ffn1 = (w_ffn1_gu[l].astype(BF16), w_ffn1_down[l].astype(BF16))
        ffn2 = (w_ffn2_gu[l].astype(BF16), w_ffn2_down[l].astype(BF16))
        merge_w = (w_gate, b_gate[l].reshape(1, GATE_DIM), w_out_a[l].astype(BF16), w_ob,
                   w_out_c[l].astype(BF16), w_out[l].astype(BF16))
        conv_w = (conv_dw_w[l], conv_dw_b[l].reshape(1, CONV_DIM), conv_ln_g[l].reshape(1, CONV_DIM),
                  conv_ln_b[l].reshape(1, CONV_DIM))
        bias_tbl = _na_bias_table(na_rpb[l])

        xs = _ffn(xs, mod_x, g[0:1], *ffn1, final_row, mod_off=0, final=False, tiles_per_mod=tiles_x)
        hc = _ffn(hc, mod_c, g[0:1], *ffn1, final_row, mod_off=0, final=False, tiles_per_mod=tiles_c)

        za_x, zv_x, zu_x = _inproj(xs, mod_x, g[1:2], w_main, cos_tbl, sin_tbl,
                                   tiles_per_mod=tiles_x, table_tiles=tiles_x)
        za_c, zv_c, zu_c = _inproj(hc, mod_c, g[1:2], w_main, cos_one, sin_zero,
                                   tiles_per_mod=tiles_c, table_tiles=1)

        ya = _na(za_x, zv_x, za_c, zv_c, bias_tbl, batch=batch, seq=seq, n_ctx=n_ctx)
        yb = _sw(za_x, zv_x, zu_x, za_c, zv_c, zu_c, _sink_rows(sw_sink[l], SW_BLOCK),
                 batch=batch, seq=seq, n_ctx=n_ctx)
        yc = _conv(zu_x, *conv_w, n_seq=batch, length=seq)
        xs = _merge(xs, mod_x, g[1:2], ya, yb, yc, *merge_w, tiles_per_mod=tiles_x)
        if not last:
            ya_c, yb_c = _ctx_attn(za_c, zv_c, zu_c, _sink_rows(sw_sink[l], n_ctx), batch=batch, n_ctx=n_ctx)
            yc_c = _conv(zu_c, *conv_w, n_seq=batch, length=n_ctx)
            hc = _merge(hc, mod_c, g[1:2], ya_c, yb_c, yc_c, *merge_w, tiles_per_mod=tiles_c)

        xs = _ffn(xs, mod_x, g[2:3], *ffn2, final_row, mod_off=6, final=last, tiles_per_mod=tiles_x)
        if not last:
            hc = _ffn(hc, mod_c, g[2:3], *ffn2, final_row, mod_off=6, final=False, tiles_per_mod=tiles_c)
    return xs.reshape(batch, seq, d)
```

```python
import functools
import math

import jax
import jax.numpy as jnp
import numpy as np
from jax import lax
from jax.experimental import pallas as pl
from jax.experimental.pallas import tpu as pltpu

D_MODEL = 1024
GRID_W = 64
HEAD_DIM = 64
NA_HEADS = 6
NA_WIN_H = 8
NA_WIN_W = 16
SW_HEADS = 6
SW_KV_HEADS = 2
SW_WINDOW = 128
SW_BLOCK = 128
CONV_DIM = 256
CONV_WIDTH = 31
D_FF = 2816
ROPE_BASE = 10000.0
EPS = 1e-6
NEG_INF = -1e30
N_MOD = 9
LOG2E = math.log2(math.e)

A_DIM = NA_HEADS * HEAD_DIM
B_Q_DIM = SW_HEADS * HEAD_DIM
B_KV_DIM = SW_KV_HEADS * HEAD_DIM
GATE_DIM = 3 * D_MODEL

LANES = 128
SUBLANES = 8
VMEM_BYTES_V7X = 64 * 1024 * 1024
VMEM_LIMIT = VMEM_BYTES_V7X - 8 * 1024 * 1024

P_QA, P_KA, P_QB = 0, A_DIM, 2 * A_DIM
P_VA = 2 * A_DIM + B_Q_DIM
P_VB = P_VA + A_DIM
P_U = P_VB + B_KV_DIM
P_KB = P_U + 2 * CONV_DIM
P_DIM = P_KB + B_KV_DIM
ZA_DIM = 2 * A_DIM + B_Q_DIM
VAUG = 2 * LANES
N_VA_TILES = A_DIM // LANES
ZV_DIM = (N_VA_TILES + 1) * VAUG
ZU_DIM = 2 * CONV_DIM + B_KV_DIM

TOKEN_TILE = 1024
TOKEN_SUBTILE = 512
FF_CHUNKS = ((0, 768), (768, 1536), (1536, 2304), (2304, 2816))
NA_ROWS_PER_STEP = 16
SW_BLOCKS_PER_STEP = 8
CONV_CHUNK = 128
CONV_HALO = 16
CONV_SPAN = CONV_CHUNK + 2 * CONV_HALO - SUBLANES

BF16 = jnp.bfloat16
F32 = jnp.float32


def _params(n_axes):
    return pltpu.CompilerParams(dimension_semantics=("arbitrary",) * n_axes,
                                vmem_limit_bytes=VMEM_LIMIT)


def _const_spec(shape):
    nd = len(shape)
    return pl.BlockSpec(shape, lambda *_: (0,) * nd, pipeline_mode=pl.Buffered(1))


def _sigmoid(v):
    return 1.0 / (1.0 + jnp.exp(-v))


def _silu(v):
    return v * _sigmoid(v)


def _norm_modulate(x, g, shift, scale):
    ms = jnp.mean(x * x, axis=-1, keepdims=True)
    y = x * lax.rsqrt(ms + EPS) * g
    return y * (1.0 + scale) + shift


def _dot(a, b):
    return jnp.dot(a, b, preferred_element_type=F32)


def _dot_nt(a, b):
    return lax.dot_general(a, b, (((1,), (1,)), ((), ())), preferred_element_type=F32)


def _ada_kernel(c_ref, w_ref, b_ref, o_ref):
    s = _silu(c_ref[...]).astype(BF16)
    o_ref[0] = _dot(s, w_ref[0].astype(BF16)) + b_ref[0]


def _ada(cvec, w_ada, b_ada):
    depth, d, n = w_ada.shape
    rows = cvec.shape[0]
    bn = 1024
    return pl.pallas_call(
        _ada_kernel,
        grid=(depth, n // bn),
        in_specs=[pl.BlockSpec((rows, d), lambda l, j: (0, 0)),
                  pl.BlockSpec((1, d, bn), lambda l, j: (l, 0, j)),
                  pl.BlockSpec((1, 1, bn), lambda l, j: (l, 0, j))],
        out_specs=pl.BlockSpec((1, rows, bn), lambda l, j: (l, 0, j)),
        out_shape=jax.ShapeDtypeStruct((depth, rows, n), F32),
        compiler_params=_params(2),
        name="ada",
    )(cvec, w_ada, b_ada.reshape(depth, 1, n))


def _ffn_kernel(x_ref, mod_ref, g_ref, wgu_ref, wd_ref, fg_ref, o_ref, *, mod_off, final):
    shift = mod_ref[0, mod_off:mod_off + 1, :]
    scale = mod_ref[0, mod_off + 1:mod_off + 2, :]
    gate = mod_ref[0, mod_off + 2:mod_off + 3, :]
    for r0 in range(0, x_ref.shape[0], TOKEN_SUBTILE):
        rows = slice(r0, r0 + TOKEN_SUBTILE)
        x = x_ref[rows, :]
        h = _norm_modulate(x, g_ref[...], shift, scale).astype(BF16)
        acc = jnp.zeros(x.shape, F32)
        for c0, c1 in FF_CHUNKS:
            a = _dot(h, wgu_ref[:, c0:c1])
            b = _dot(h, wgu_ref[:, D_FF + c0:D_FF + c1])
            act = (_silu(a) * b).astype(BF16)
            acc = acc + _dot(act, wd_ref[c0:c1, :])
        y = x + (0.5 * gate) * acc
        if final:
            ms = jnp.mean(y * y, axis=-1, keepdims=True)
            y = y * lax.rsqrt(ms + EPS) * fg_ref[...]
        o_ref[rows, :] = y


def _ffn(x, mod, g, w_gu, w_down, final_g, *, mod_off, final, tiles_per_mod):
    t, d = x.shape
    tm = TOKEN_TILE
    return pl.pallas_call(
        functools.partial(_ffn_kernel, mod_off=mod_off, final=final),
        grid=(t // tm,),
        in_specs=[pl.BlockSpec((tm, d), lambda i: (i, 0)),
                  pl.BlockSpec((1, N_MOD, d), lambda i: (i // tiles_per_mod, 0, 0)),
                  _const_spec((1, d)),
                  _const_spec(w_gu.shape),
                  _const_spec(w_down.shape),
                  _const_spec((1, d))],
        out_specs=pl.BlockSpec((tm, d), lambda i: (i, 0)),
        out_shape=jax.ShapeDtypeStruct((t, d), F32),
        compiler_params=_params(1),
        name="ffn",
    )(x, mod, g, w_gu, w_down, final_g)


def _rope_tile(t, cos, sin_signed, first_half):
    fwd = pltpu.roll(t, LANES - HEAD_DIM // 2, axis=1)
    bwd = pltpu.roll(t, HEAD_DIM // 2, axis=1)
    return t * cos + jnp.where(first_half, fwd, bwd) * sin_signed


def _inproj_kernel(x_ref, mod_ref, g_ref, w_ref, cos_ref, sin_ref, za_ref, zv_ref, zu_ref):
    lane = lax.broadcasted_iota(jnp.int32, (1, LANES), 1)
    first_half = (lane % HEAD_DIM) < (HEAD_DIM // 2)
    ones = jnp.ones((TOKEN_SUBTILE, LANES), BF16)
    for r0 in range(0, x_ref.shape[0], TOKEN_SUBTILE):
        rows = slice(r0, r0 + TOKEN_SUBTILE)
        x = x_ref[rows, :]
        h = _norm_modulate(x, g_ref[...], mod_ref[0, 3:4, :], mod_ref[0, 4:5, :]).astype(BF16)
        z = _dot(h, w_ref[...])
        cos = cos_ref[rows, :]
        sin = sin_ref[rows, :]
        za_ref[rows, 0:P_QB] = z[:, 0:P_QB].astype(BF16)
        for c0 in range(0, B_Q_DIM, LANES):
            za_ref[rows, P_QB + c0:P_QB + c0 + LANES] = _rope_tile(
                z[:, P_QB + c0:P_QB + c0 + LANES], cos, sin, first_half).astype(BF16)
        for j in range(N_VA_TILES + 1):
            zv_ref[rows, j * VAUG:j * VAUG + LANES] = z[:, P_VA + j * LANES:P_VA + (j + 1) * LANES].astype(BF16)
            zv_ref[rows, j * VAUG + LANES:(j + 1) * VAUG] = ones
        zu_ref[rows, 0:2 * CONV_DIM] = z[:, P_U:P_KB].astype(BF16)
        zu_ref[rows, 2 * CONV_DIM:ZU_DIM] = _rope_tile(z[:, P_KB:P_DIM], cos, sin, first_half).astype(BF16)


def _inproj(x, mod, g, w_main, cos_tbl, sin_tbl, *, tiles_per_mod, table_tiles):
    t, d = x.shape
    tm = TOKEN_TILE
    tok = lambda width: pl.BlockSpec((tm, width), lambda i: (i, 0))
    return pl.pallas_call(
        _inproj_kernel,
        grid=(t // tm,),
        in_specs=[tok(d),
                  pl.BlockSpec((1, N_MOD, d), lambda i: (i // tiles_per_mod, 0, 0)),
                  _const_spec((1, d)),
                  _const_spec(w_main.shape),
                  pl.BlockSpec((tm, LANES), lambda i: (i % table_tiles, 0)),
                  pl.BlockSpec((tm, LANES), lambda i: (i % table_tiles, 0))],
        out_specs=[tok(ZA_DIM), tok(ZV_DIM), tok(ZU_DIM)],
        out_shape=[jax.ShapeDtypeStruct((t, ZA_DIM), BF16),
                   jax.ShapeDtypeStruct((t, ZV_DIM), BF16),
                   jax.ShapeDtypeStruct((t, ZU_DIM), BF16)],
        compiler_params=_params(1),
        name="inproj",
    )(x, mod, g, w_main, cos_tbl, sin_tbl)


def _half_masks(dtype):
    lane = lax.broadcasted_iota(jnp.int32, (1, LANES), 1)
    lo = (lane < HEAD_DIM).astype(F32).astype(dtype)
    hi = (lane >= HEAD_DIM).astype(F32).astype(dtype)
    return lo, hi


def _lane_tiles(s):
    return [s[:, c:c + LANES] for c in range(0, s.shape[1], LANES)]


def _row_max(s_list):
    tiles = [t for s in s_list for t in _lane_tiles(s)]
    return functools.reduce(jnp.maximum, tiles).max(axis=-1, keepdims=True)


def _exp2_bf16(s, m):
    return jnp.concatenate([jnp.exp2(t - m).astype(BF16) for t in _lane_tiles(s)], axis=1)


def _merge_halves(o, n_tiles, rows):
    lane = lax.broadcasted_iota(jnp.int32, (1, LANES), 1)
    low = lane < HEAD_DIM
    return [jnp.where(low, o[j * rows:(j + 1) * rows], o[(n_tiles + j) * rows:(n_tiles + j + 1) * rows])
            for j in range(n_tiles)]


def _stack_pair(qt, lo, hi):
    return jnp.concatenate([qt * lo, qt * hi], axis=0)


def _stack_group_queries(q, lo, hi):
    tiles = _lane_tiles(q)
    return jnp.concatenate([t * lo for t in tiles] + [t * hi for t in tiles], axis=0)


def _na_kernel(q_ref, k_ref, v_ref, kc_ref, vc_ref, bias_ref, o_ref, *, rows_total):
    lo, hi = _half_masks(BF16)
    step = pl.program_id(1)
    n_key = NA_WIN_H * GRID_W
    pair_rows = 2 * GRID_W
    for t in range(N_VA_TILES):
        cols = slice(t * LANES, (t + 1) * LANES)
        vcols = slice(t * VAUG, (t + 1) * VAUG)
        q2 = [_stack_pair(q_ref[rr * GRID_W:(rr + 1) * GRID_W, cols], lo, hi)
              for rr in range(NA_ROWS_PER_STEP)]
        s_ctx_all = _dot_nt(jnp.concatenate(q2, axis=0), kc_ref[:, cols])
        pv_loc, p_ctx = [], []
        for rr in range(NA_ROWS_PER_STEP):
            r = step * NA_ROWS_PER_STEP + rr
            r0 = jnp.clip(r - NA_WIN_H // 2, 0, rows_total - NA_WIN_H)
            d = r - r0
            ks = pl.multiple_of(r0 * GRID_W, GRID_W)
            bias = jnp.concatenate([bias_ref[2 * t, d], bias_ref[2 * t + 1, d]], axis=0)
            s_loc = _dot_nt(q2[rr], k_ref[pl.ds(ks, n_key), cols]) + bias
            s_ctx = s_ctx_all[rr * pair_rows:(rr + 1) * pair_rows]
            m = _row_max([s_loc, s_ctx])
            pv_loc.append(_dot(_exp2_bf16(s_loc, m), v_ref[pl.ds(ks, n_key), vcols]))
            p_ctx.append(_exp2_bf16(s_ctx, m))
        acc = jnp.concatenate(pv_loc, axis=0) + _dot(jnp.concatenate(p_ctx, axis=0), vc_ref[:, vcols])
        o = acc[:, :LANES] / acc[:, LANES:]
        for rr in range(NA_ROWS_PER_STEP):
            tile = _merge_halves(o[rr * pair_rows:(rr + 1) * pair_rows], 1, GRID_W)[0]
            o_ref[rr * GRID_W:(rr + 1) * GRID_W, cols] = tile.astype(BF16)


def _na(za_x, zv_x, za_c, zv_c, bias_tbl, *, batch, seq, n_ctx):
    rows_total = seq // GRID_W
    steps = rows_total // NA_ROWS_PER_STEP
    tq = NA_ROWS_PER_STEP * GRID_W
    va_w = N_VA_TILES * VAUG
    return pl.pallas_call(
        functools.partial(_na_kernel, rows_total=rows_total),
        grid=(batch, steps),
        in_specs=[pl.BlockSpec((tq, A_DIM), lambda b, i: (b * steps + i, P_QA // A_DIM)),
                  pl.BlockSpec((seq, A_DIM), lambda b, i: (b, P_KA // A_DIM)),
                  pl.BlockSpec((seq, va_w), lambda b, i: (b, 0)),
                  pl.BlockSpec((n_ctx, A_DIM), lambda b, i: (b, P_KA // A_DIM)),
                  pl.BlockSpec((n_ctx, va_w), lambda b, i: (b, 0)),
                  _const_spec(bias_tbl.shape)],
        out_specs=pl.BlockSpec((tq, A_DIM), lambda b, i: (b * steps + i, 0)),
        out_shape=jax.ShapeDtypeStruct((batch * seq, A_DIM), BF16),
        compiler_params=_params(2),
        name="na_attn",
    )(za_x, za_x, zv_x, za_c, zv_c, bias_tbl)


def _sw_kernel(q_ref, k_ref, v_ref, kc_ref, vc_ref, sink_ref, o_ref, *, seq):
    lo, hi = _half_masks(BF16)
    step = pl.program_id(1)
    n_tiles = B_Q_DIM // LANES
    n_key = 3 * SW_BLOCK
    grp = SW_HEADS * SW_BLOCK
    sink = sink_ref[...]
    q6 = [_stack_group_queries(q_ref[j * SW_BLOCK:(j + 1) * SW_BLOCK, :], lo, hi)
          for j in range(SW_BLOCKS_PER_STEP)]
    s_ctx_all = _dot_nt(jnp.concatenate(q6, axis=0), kc_ref[...])
    pv_loc, p_ctx, p_sink = [], [], []
    for j in range(SW_BLOCKS_PER_STEP):
        n = step * SW_BLOCKS_PER_STEP + j
        start = pl.multiple_of(jnp.clip((n - 1) * SW_BLOCK, 0, seq - n_key), SW_BLOCK)
        s = _dot_nt(q6[j], k_ref[pl.ds(start, n_key), :])
        qpos = n * SW_BLOCK + lax.broadcasted_iota(jnp.int32, (SW_BLOCK, n_key), 0)
        kpos = start + lax.broadcasted_iota(jnp.int32, (SW_BLOCK, n_key), 1)
        valid = jnp.abs(qpos - kpos) <= SW_WINDOW
        s = jnp.where(valid[None], s.reshape(SW_HEADS, SW_BLOCK, n_key), NEG_INF).reshape(grp, n_key)
        s_ctx = s_ctx_all[j * grp:(j + 1) * grp]
        m = jnp.maximum(_row_max([s, s_ctx]), sink)
        pv_loc.append(_dot(_exp2_bf16(s, m), v_ref[pl.ds(start, n_key), :]))
        p_ctx.append(_exp2_bf16(s_ctx, m))
        p_sink.append(jnp.exp2(sink - m))
    acc = jnp.concatenate(pv_loc, axis=0) + _dot(jnp.concatenate(p_ctx, axis=0), vc_ref[...])
    o = acc[:, :LANES] / (acc[:, LANES:] + jnp.concatenate(p_sink, axis=0))
    for j in range(SW_BLOCKS_PER_STEP):
        for t, tile in enumerate(_merge_halves(o[j * grp:(j + 1) * grp], n_tiles, SW_BLOCK)):
            o_ref[j * SW_BLOCK:(j + 1) * SW_BLOCK, t * LANES:(t + 1) * LANES] = tile.astype(BF16)


def _sw(za_x, zv_x, zu_x, za_c, zv_c, zu_c, sink_rep, *, batch, seq, n_ctx):
    tq = SW_BLOCK * SW_BLOCKS_PER_STEP
    nb = seq // tq
    kb_blk = (2 * CONV_DIM) // B_KV_DIM
    return pl.pallas_call(
        functools.partial(_sw_kernel, seq=seq),
        grid=(batch, nb),
        in_specs=[pl.BlockSpec((tq, B_Q_DIM), lambda b, n: (b * nb + n, P_QB // B_Q_DIM)),
                  pl.BlockSpec((seq, B_KV_DIM), lambda b, n: (b, kb_blk)),
                  pl.BlockSpec((seq, VAUG), lambda b, n: (b, N_VA_TILES)),
                  pl.BlockSpec((n_ctx, B_KV_DIM), lambda b, n: (b, kb_blk)),
                  pl.BlockSpec((n_ctx, VAUG), lambda b, n: (b, N_VA_TILES)),
                  _const_spec(sink_rep.shape)],
        out_specs=pl.BlockSpec((tq, B_Q_DIM), lambda b, n: (b * nb + n, 0)),
        out_shape=jax.ShapeDtypeStruct((batch * seq, B_Q_DIM), BF16),
        compiler_params=_params(2),
        name="sw_attn",
    )(za_x, zu_x, zv_x, zu_c, zv_c, sink_rep)


def _ctx_attn_kernel(qa_ref, ka_ref, va_ref, qb_ref, kb_ref, vb_ref, sink_ref, oa_ref, ob_ref):
    lo, hi = _half_masks(BF16)
    n_ctx = qa_ref.shape[0]
    for t in range(N_VA_TILES):
        cols = slice(t * LANES, (t + 1) * LANES)
        s = _dot_nt(_stack_pair(qa_ref[:, cols], lo, hi), ka_ref[:, cols])
        acc = _dot(_exp2_bf16(s, _row_max([s])), va_ref[:, t * VAUG:(t + 1) * VAUG])
        o = acc[:, :LANES] / acc[:, LANES:]
        oa_ref[:, cols] = _merge_halves(o, 1, n_ctx)[0].astype(BF16)
    sink = sink_ref[...]
    s = _dot_nt(_stack_group_queries(qb_ref[...], lo, hi), kb_ref[...])
    m = jnp.maximum(_row_max([s]), sink)
    acc = _dot(_exp2_bf16(s, m), vb_ref[...])
    o = acc[:, :LANES] / (acc[:, LANES:] + jnp.exp2(sink - m))
    for t, tile in enumerate(_merge_halves(o, B_Q_DIM // LANES, n_ctx)):
        ob_ref[:, t * LANES:(t + 1) * LANES] = tile.astype(BF16)


def _ctx_attn(za_c, zv_c, zu_c, sink_rep, *, batch, n_ctx):
    spec = lambda width, blk: pl.BlockSpec((n_ctx, width), lambda b: (b, blk))
    return pl.pallas_call(
        _ctx_attn_kernel,
        grid=(batch,),
        in_specs=[spec(A_DIM, P_QA // A_DIM), spec(A_DIM, P_KA // A_DIM), spec(N_VA_TILES * VAUG, 0),
                  spec(B_Q_DIM, P_QB // B_Q_DIM), spec(B_KV_DIM, (2 * CONV_DIM) // B_KV_DIM),
                  spec(VAUG, N_VA_TILES),
                  _const_spec(sink_rep.shape)],
        out_specs=[pl.BlockSpec((n_ctx, A_DIM), lambda b: (b, 0)),
                   pl.BlockSpec((n_ctx, B_Q_DIM), lambda b: (b, 0))],
        out_shape=[jax.ShapeDtypeStruct((batch * n_ctx, A_DIM), BF16),
                   jax.ShapeDtypeStruct((batch * n_ctx, B_Q_DIM), BF16)],
        compiler_params=_params(1),
        name="ctx_attn",
    )(za_c, za_c, zv_c, za_c, zu_c, zv_c, sink_rep)


def _conv_kernel(u_ref, w_ref, b_ref, g_ref, beta_ref, o_ref, hp_ref, sh_ref):
    length = u_ref.shape[0]
    hp_ref[0:CONV_HALO, :] = jnp.zeros((CONV_HALO, CONV_DIM), F32)
    hp_ref[CONV_HALO + length:CONV_HALO + length + CONV_HALO, :] = jnp.zeros((CONV_HALO, CONV_DIM), F32)

    def glu_body(c, carry):
        r0 = pl.multiple_of(c * CONV_CHUNK, CONV_CHUNK)
        u = u_ref[pl.ds(r0, CONV_CHUNK), :].astype(F32)
        hp_ref[pl.ds(CONV_HALO + r0, CONV_CHUNK), :] = u[:, :CONV_DIM] * _sigmoid(u[:, CONV_DIM:])
        return carry

    lax.fori_loop(0, length // CONV_CHUNK, glu_body, 0)

    def conv_body(c, carry):
        r0 = pl.multiple_of(c * CONV_CHUNK, CONV_CHUNK)
        win = hp_ref[pl.ds(r0, CONV_CHUNK + 2 * CONV_HALO), :]
        lead = CONV_HALO - CONV_WIDTH // 2
        for res in range(SUBLANES):
            sh_ref[res] = win[res:res + CONV_SPAN]
        acc = jnp.zeros((CONV_CHUNK, CONV_DIM), F32)
        for k in range(CONV_WIDTH):
            res = (k + lead) % SUBLANES
            off = k + lead - res
            acc = acc + sh_ref[res, off:off + CONV_CHUNK, :] * w_ref[k:k + 1, :]
        acc = acc + b_ref[...]
        mu = jnp.mean(acc, axis=-1, keepdims=True)
        cen = acc - mu
        var = jnp.mean(cen * cen, axis=-1, keepdims=True)
        y = cen * lax.rsqrt(var + EPS) * g_ref[...] + beta_ref[...]
        o_ref[pl.ds(r0, CONV_CHUNK), :] = _silu(y).astype(BF16)
        return carry

    lax.fori_loop(0, length // CONV_CHUNK, conv_body, 0, unroll=2)


def _conv(zu, w, b, g, beta, *, n_seq, length):
    return pl.pallas_call(
        _conv_kernel,
        grid=(n_seq,),
        in_specs=[pl.BlockSpec((length, 2 * CONV_DIM), lambda s: (s, 0)),
                  _const_spec(w.shape), _const_spec(b.shape), _const_spec(g.shape), _const_spec(beta.shape)],
        out_specs=pl.BlockSpec((length, CONV_DIM), lambda s: (s, 0)),
        out_shape=jax.ShapeDtypeStruct((n_seq * length, CONV_DIM), BF16),
        scratch_shapes=[pltpu.VMEM((length + 2 * CONV_HALO, CONV_DIM), F32),
                        pltpu.VMEM((SUBLANES, CONV_SPAN, CONV_DIM), F32)],
        compiler_params=_params(1),
        name="conv",
    )(zu, w, b, g, beta)


def _merge_kernel(x_ref, mod_ref, g_ref, ya_ref, yb_ref, yc_ref, wg_ref, bg_ref,
                  woa_ref, wob_ref, woc_ref, wo_ref, o_ref):
    d = x_ref.shape[1]
    branches = ((ya_ref, woa_ref), (yb_ref, wob_ref), (yc_ref, woc_ref))
    for r0 in range(0, x_ref.shape[0], TOKEN_SUBTILE):
        rows = slice(r0, r0 + TOKEN_SUBTILE)
        x = x_ref[rows, :]
        h = _norm_modulate(x, g_ref[...], mod_ref[0, 3:4, :], mod_ref[0, 4:5, :]).astype(BF16)
        mix = None
        for j, (y_ref, w_ref) in enumerate(branches):
            logits = _dot(h, wg_ref[:, j * d:(j + 1) * d]) + bg_ref[:, j * d:(j + 1) * d]
            term = _sigmoid(logits) * _dot(y_ref[rows, :], w_ref[...])
            mix = term if mix is None else mix + term
        out = _dot(mix.astype(BF16), wo_ref[...])
        o_ref[rows, :] = x + mod_ref[0, 5:6, :] * out


def _merge(x, mod, g, ya, yb, yc, w_gate, b_gate, w_oa, w_ob, w_oc, w_o, *, tiles_per_mod):
    t, d = x.shape
    tm = TOKEN_TILE
    tok = lambda width: pl.BlockSpec((tm, width), lambda i: (i, 0))
    return pl.pallas_call(
        _merge_kernel,
        grid=(t // tm,),
        in_specs=[tok(d),
                  pl.BlockSpec((1, N_MOD, d), lambda i: (i // tiles_per_mod, 0, 0)),
                  _const_spec((1, d)),
                  tok(A_DIM), tok(B_Q_DIM), tok(CONV_DIM),
                  _const_spec(w_gate.shape), _const_spec(b_gate.shape),
                  _const_spec(w_oa.shape), _const_spec(w_ob.shape), _const_spec(w_oc.shape),
                  _const_spec(w_o.shape)],
        out_specs=tok(d),
        out_shape=jax.ShapeDtypeStruct((t, d), F32),
        compiler_params=_params(1),
        name="merge",
    )(x, mod, g, ya, yb, yc, w_gate, b_gate, w_oa, w_ob, w_oc, w_o)


def _rope_tables(seq):
    t = jnp.arange(seq, dtype=jnp.int32)
    row = (t // GRID_W).astype(F32)
    col = (t % GRID_W).astype(F32)
    n_freq = HEAD_DIM // 4
    inv_freq = ROPE_BASE ** (-jnp.arange(n_freq, dtype=F32) / n_freq)
    ang = jnp.concatenate([row[:, None] * inv_freq, col[:, None] * inv_freq], axis=-1)
    cos, sin = jnp.cos(ang), jnp.sin(ang)
    reps = LANES // (HEAD_DIM // 2)
    cos_tbl = jnp.tile(cos, (1, reps))
    sin_tbl = jnp.tile(jnp.concatenate([-sin, sin], axis=-1), (1, reps // 2))
    return cos_tbl, sin_tbl


def _na_bias_table(rpb):
    heads, n_dy, _ = rpb.shape
    n = GRID_W
    qc = np.arange(n)
    kc = np.arange(n)
    win_c0 = np.clip(qc - NA_WIN_W // 2, 0, n - NA_WIN_W)[:, None]
    in_win = (kc[None, :] >= win_c0) & (kc[None, :] < win_c0 + NA_WIN_W)
    pad = (n - 1) - (NA_WIN_W - 1)
    p = jnp.pad(rpb.astype(F32) * LOG2E, ((0, 0), (0, 0), (pad, pad + 1)))
    flat = jnp.tile(p, (1, 1, n))[:, :, :n * (2 * n - 1)]
    toe = flat.reshape(heads, n_dy, n, 2 * n - 1)[:, :, :, n - 1:]
    toe = jnp.where(in_win[None, None], toe, NEG_INF)
    per_d = [toe[:, NA_WIN_H - 1 - d:2 * NA_WIN_H - 1 - d].transpose(0, 2, 1, 3)
             .reshape(heads, n, NA_WIN_H * n) for d in range(NA_WIN_H)]
    return jnp.stack(per_d, axis=1)


_SW_HEAD_ORDER = (0, 3, 1, 4, 2, 5)


def _sw_head_perm():
    return np.concatenate([np.arange(h * HEAD_DIM, (h + 1) * HEAD_DIM) for h in _SW_HEAD_ORDER])


def _layer_weights(l, w_in, w_out_b):
    a, bq, bkv, cv = A_DIM, B_Q_DIM, B_KV_DIM, 2 * CONV_DIM
    q_scale = HEAD_DIM ** -0.5 * LOG2E
    w = w_in[l]
    o_qb = 3 * a
    o_kb = o_qb + bq
    o_vb = o_kb + bkv
    o_u = o_vb + bkv
    o_gate = o_u + cv
    perm = _sw_head_perm()
    w_main = jnp.concatenate([
        w[:, 0:a] * q_scale, w[:, a:2 * a],
        w[:, o_qb:o_kb][:, perm] * q_scale,
        w[:, 2 * a:3 * a], w[:, o_vb:o_u],
        w[:, o_u:o_gate], w[:, o_kb:o_vb]], axis=1).astype(BF16)
    w_gate = w[:, o_gate:].astype(BF16)
    w_ob = w_out_b[l][perm, :].astype(BF16)
    return w_main, w_gate, w_ob


def _sink_rows(sink, rows_per_head):
    col = jnp.repeat(sink.astype(F32) * LOG2E, rows_per_head)
    return jnp.broadcast_to(col[:, None], (col.shape[0], LANES))


def kernel(x, c, ctx, c_ctx, w_ada, b_ada, norm_g, w_ffn1_gu, w_ffn1_down, w_ffn2_gu, w_ffn2_down, w_in, b_gate, na_rpb, sw_sink, conv_dw_w, conv_dw_b, conv_ln_g, conv_ln_b, w_out_a, w_out_b, w_out_c, w_out, final_g):
    batch, seq, d = x.shape
    n_ctx = ctx.shape[1]
    depth = w_ada.shape[0]
    assert seq % TOKEN_TILE == 0 and (batch * n_ctx) % TOKEN_TILE == 0
    assert seq % (NA_ROWS_PER_STEP * GRID_W) == 0 and seq % (SW_BLOCKS_PER_STEP * SW_BLOCK) == 0
    assert seq % CONV_CHUNK == 0 and n_ctx % (2 * CONV_CHUNK) == 0 and n_ctx % LANES == 0
    tiles_x = seq // TOKEN_TILE
    tiles_c = (batch * n_ctx) // TOKEN_TILE

    pad_rows = (-(batch + 1)) % SUBLANES
    cvec = jnp.concatenate([c, c_ctx[None, :], jnp.zeros((pad_rows, d), F32)], axis=0)
    mods = _ada(cvec, w_ada, b_ada).reshape(depth, batch + 1 + pad_rows, N_MOD, d)

    cos_tbl, sin_tbl = _rope_tables(seq)
    cos_one = jnp.ones((TOKEN_TILE, LANES), F32)
    sin_zero = jnp.zeros((TOKEN_TILE, LANES), F32)
    final_row = final_g.reshape(1, d)

    xs = x.reshape(batch * seq, d)
    hc = ctx.reshape(batch * n_ctx, d)
    for l in range(depth):
        last = l == depth - 1
        mod_x = mods[l, :batch]
        mod_c = mods[l, batch:batch + 1]
        g = norm_g[l]
        w_main, w_gate, w_ob = _layer_weights(l, w_in, w_out_b)
        ffn1 = (w_ffn1_gu[l].astype(BF16), w_ffn1_down[l].astype(BF16))
        ffn2 = (w_ffn2_gu[l].astype(BF16), w_ffn2_down[l].astype(BF16))
        merge_w = (w_gate, b_gate[l].reshape(1, GATE_DIM), w_out_a[l].astype(BF16), w_ob,
                   w_out_c[l].astype(BF16), w_out[l].astype(BF16))
        conv_w = (conv_dw_w[l], conv_dw_b[l].reshape(1, CONV_DIM), conv_ln_g[l].reshape(1, CONV_DIM),
                  conv_ln_b[l].reshape(1, CONV_DIM))
        bias_tbl = _na_bias_table(na_rpb[l])

        xs = _ffn(xs, mod_x, g[0:1], *ffn1, final_row, mod_off=0, final=False, tiles_per_mod=tiles_x)
        hc = _ffn(hc, mod_c, g[0:1], *ffn1, final_row, mod_off=0, final=False, tiles_per_mod=tiles_c)

        za_x, zv_x, zu_x = _inproj(xs, mod_x, g[1:2], w_main, cos_tbl, sin_tbl,
                                   tiles_per_mod=tiles_x, table_tiles=tiles_x)
        za_c, zv_c, zu_c = _inproj(hc, mod_c, g[1:2], w_main, cos_one, sin_zero,
                                   tiles_per_mod=tiles_c, table_tiles=1)

        ya = _na(za_x, zv_x, za_c, zv_c, bias_tbl, batch=batch, seq=seq, n_ctx=n_ctx)
        yb = _sw(za_x, zv_x, zu_x, za_c, zv_c, zu_c, _sink_rows(sw_sink[l], SW_BLOCK),
                 batch=batch, seq=seq, n_ctx=n_ctx)
        yc = _conv(zu_x, *conv_w, n_seq=batch, length=seq)
        xs = _merge(xs, mod_x, g[1:2], ya, yb, yc, *merge_w, tiles_per_mod=tiles_x)
        if not last:
            ya_c, yb_c = _ctx_attn(za_c, zv_c, zu_c, _sink_rows(sw_sink[l], n_ctx), batch=batch, n_ctx=n_ctx)
            yc_c = _conv(zu_c, *conv_w, n_seq=batch, length=n_ctx)
            hc = _merge(hc, mod_c, g[1:2], ya_c, yb_c, yc_c, *merge_w, tiles_per_mod=tiles_c)

        xs = _ffn(xs, mod_x, g[2:3], *ffn2, final_row, mod_off=6, final=last, tiles_per_mod=tiles_x)
        if not last:
            hc = _ffn(hc, mod_c, g[2:3], *ffn2, final_row, mod_off=6, final=False, tiles_per_mod=tiles_c)
    return xs.reshape(batch, seq, d)
```
